```python
import math
import jax, jax.numpy as jnp
from jax import lax
import numpy as np

D_MODEL = 2048
BATCH = 8
SEQ = 2048
DEPTH = 2

EPS = 1e-6
ROPE_THETA = 10000.0
Q_BLOCK = 128
NEG_INF = -1e30

MLA_HEADS = D_MODEL // 128
MLA_Q_LORA = D_MODEL // 4
MLA_KV_LORA = D_MODEL // 4
MLA_NOPE = 128
MLA_ROPE = 64
MLA_V = 128
MLA_DOWN = MLA_Q_LORA + MLA_KV_LORA + MLA_ROPE

DIL_PAIRS = ((128, 1), (512, 4), (2048, 16))
DIL_GROUPS = len(DIL_PAIRS)
DIL_HEADS = D_MODEL // 128
DIL_HEAD_DIM = 128

FFN_HIDDEN = ((8 * D_MODEL + 3 * 256 - 1) // (3 * 256)) * 256

kernel_name = "hybrid_mla_dilated_swiglu"


def rms_norm(x, gain):
    xf = x.astype(jnp.float32)
    y = xf * lax.rsqrt(jnp.mean(xf * xf, axis=-1, keepdims=True) + EPS)
    return (y * gain.astype(jnp.float32)).astype(x.dtype)


def apply_rope(x, positions):
    dim = x.shape[-1]
    half = dim // 2
    inv_freq = jnp.power(ROPE_THETA, -2.0 * jnp.arange(half, dtype=jnp.float32) / dim)
    ang = positions.astype(jnp.float32)[..., None] * inv_freq
    ang = ang.reshape(ang.shape[:2] + (1,) * (x.ndim - 3) + (half,))
    cos, sin = jnp.cos(ang), jnp.sin(ang)
    xf = x.astype(jnp.float32)
    x1, x2 = xf[..., :half], xf[..., half:]
    return jnp.concatenate([x1 * cos - x2 * sin, x2 * cos + x1 * sin], axis=-1).astype(x.dtype)


def causal_block_attention(q, k, v, scale):
    B, S, H, Dk = q.shape
    Dv = v.shape[-1]
    nb = S // Q_BLOCK
    qb = q.reshape(B, nb, Q_BLOCK, H, Dk).transpose(1, 0, 2, 3, 4)
    kpos = jnp.arange(S)

    def one_block(args):
        qblk, start = args
        s = jnp.einsum('bqhd,bkhd->bhqk', qblk, k, preferred_element_type=jnp.float32) * scale
        qpos = start + jnp.arange(Q_BLOCK)
        mask = kpos[None, :] <= qpos[:, None]
        s = jnp.where(mask[None, None], s, NEG_INF)
        p = jax.nn.softmax(s, axis=-1)
        return jnp.einsum('bhqk,bkhd->bqhd', p.astype(v.dtype), v)

    o = lax.map(one_block, (qb, jnp.arange(nb) * Q_BLOCK))
    return o.transpose(1, 0, 2, 3, 4).reshape(B, S, H, Dv)


def strided_window_attention(q, k, v, dilation, span, scale):
    B, S, H, Dh = q.shape
    L = S // dilation
    N = B * dilation

    def to_residue(t):
        return t.reshape(B, L, dilation, H, Dh).transpose(0, 2, 1, 3, 4).reshape(N, L, H, Dh)

    qr, kr, vr = to_residue(q), to_residue(k), to_residue(v)
    blk = math.gcd(L, Q_BLOCK)
    nb = L // blk
    slab = span + blk
    pad = ((0, 0), (span, 0), (0, 0), (0, 0))
    kp, vp = jnp.pad(kr, pad), jnp.pad(vr, pad)
    idx = jnp.arange(nb)[:, None] * blk + jnp.arange(slab)[None, :]
    kb, vb = kp[:, idx], vp[:, idx]
    qb = qr.reshape(N, nb, blk, H, Dh)
    s = jnp.einsum('nbqhd,nbkhd->nbhqk', qb, kb, preferred_element_type=jnp.float32) * scale
    i = jnp.arange(blk)[:, None]
    j = jnp.arange(slab)[None, :]
    dist = i + span - j
    key_pos = jnp.arange(nb)[:, None, None] * blk + j[None] - span
    valid = (dist >= 0) & (dist <= span) & (key_pos >= 0)
    s = jnp.where(valid[None, :, None], s, NEG_INF)
    m = jnp.max(s, axis=-1, keepdims=True)
    p = jnp.exp(s - m)
    l = jnp.sum(p, axis=-1, keepdims=True)
    o = jnp.einsum('nbhqk,nbkhd->nbqhd', (p / l).astype(v.dtype), vb)
    lse = (m + jnp.log(l))[..., 0]
    o = o.reshape(B, dilation, L, H, Dh).transpose(0, 2, 1, 3, 4).reshape(B, S, H, Dh)
    lse = lse.transpose(0, 1, 3, 2).reshape(B, dilation, L, H).transpose(0, 2, 1, 3).reshape(B, S, H)
    return o, lse


def mla_mixer(h, positions, w_down, q_norm, kv_norm, w_uq, w_ukv, q_gain, k_gain, w_o):
    B, S, _ = h.shape
    down = jnp.einsum('bsd,de->bse', h, w_down)
    c_q = rms_norm(down[..., :MLA_Q_LORA], q_norm)
    c_kv = rms_norm(down[..., MLA_Q_LORA:MLA_Q_LORA + MLA_KV_LORA], kv_norm)
    k_rope_raw = down[..., MLA_Q_LORA + MLA_KV_LORA:]
    q = jnp.einsum('bsr,re->bse', c_q, w_uq).reshape(B, S, MLA_HEADS, MLA_NOPE + MLA_ROPE)
    kv = jnp.einsum('bsr,re->bse', c_kv, w_ukv).reshape(B, S, MLA_HEADS, MLA_NOPE + MLA_V)
    q_nope = rms_norm(q[..., :MLA_NOPE], q_gain[:MLA_NOPE])
    q_rope = apply_rope(rms_norm(q[..., MLA_NOPE:], q_gain[MLA_NOPE:]), positions)
    k_nope = rms_norm(kv[..., :MLA_NOPE], k_gain[:MLA_NOPE])
    k_rope = apply_rope(rms_norm(k_rope_raw, k_gain[MLA_NOPE:]), positions)
    v = kv[..., MLA_NOPE:]
    q = jnp.concatenate([q_nope, q_rope], axis=-1)
    k = jnp.concatenate([k_nope, jnp.broadcast_to(k_rope[:, :, None, :], (B, S, MLA_HEADS, MLA_ROPE))], axis=-1)
    o = causal_block_attention(q, k, v, 1.0 / math.sqrt(MLA_NOPE + MLA_ROPE))
    return jnp.einsum('bse,ed->bsd', o.reshape(B, S, MLA_HEADS * MLA_V), w_o)


def dilated_mixer(h, positions, w_qkv, q_gain, k_gain, w_o):
    B, S, _ = h.shape
    qkv = jnp.einsum('bsd,de->bse', h, w_qkv).reshape(B, S, 3, DIL_GROUPS, DIL_HEADS, DIL_HEAD_DIM)
    q = apply_rope(rms_norm(qkv[:, :, 0], q_gain[:, None, :]), positions)
    k = apply_rope(rms_norm(qkv[:, :, 1], k_gain[:, None, :]), positions)
    v = qkv[:, :, 2]
    scale = 1.0 / math.sqrt(DIL_HEAD_DIM)
    outs, lses = [], []
    for g, (window, dilation) in enumerate(DIL_PAIRS):
        o_g, lse_g = strided_window_attention(q[:, :, g], k[:, :, g], v[:, :, g], dilation, window // dilation, scale)
        outs.append(o_g)
        lses.append(lse_g)
    wts = jax.nn.softmax(jnp.stack(lses, axis=0), axis=0)
    o = jnp.sum(wts[..., None] * jnp.stack(outs, axis=0).astype(jnp.float32), axis=0).astype(h.dtype)
    return jnp.einsum('bse,ed->bsd', o.reshape(B, S, DIL_HEADS * DIL_HEAD_DIM), w_o)


def swiglu(h, w_gate, w_up, w_down):
    a = jax.nn.silu(jnp.einsum('bsd,df->bsf', h, w_gate)) * jnp.einsum('bsd,df->bsf', h, w_up)
    return jnp.einsum('bsf,fd->bsd', a, w_down)


def setup_inputs(seed: int = 0) -> dict:
    key = jax.random.key(seed)
    ks = jax.random.split(key, 24)
    n_a = (DEPTH + 1) // 2
    n_b = DEPTH // 2
    f32 = jnp.float32

    def nrm(k, shape, fan_in):
        return jax.random.normal(k, shape, f32) * (fan_in ** -0.5)

    def gain(k, shape):
        return 1.0 + 0.05 * jax.random.normal(k, shape, f32)

    x = jax.random.normal(ks[0], (BATCH, SEQ, D_MODEL), f32)
    offsets = jax.random.randint(ks[1], (BATCH, 1), 0, 4096, dtype=jnp.int32)
    positions = offsets + jnp.arange(SEQ, dtype=jnp.int32)[None, :]
    return {
        "x": x,
        "positions": positions,
        "mixer_norm": gain(ks[2], (DEPTH, D_MODEL)),
        "ffn_norm": gain(ks[3], (DEPTH, D_MODEL)),
        "mla_w_down": nrm(ks[4], (n_a, D_MODEL, MLA_DOWN), D_MODEL),
        "mla_q_norm": gain(ks[5], (n_a, MLA_Q_LORA)),
        "mla_kv_norm": gain(ks[6], (n_a, MLA_KV_LORA)),
        "mla_w_uq": nrm(ks[7], (n_a, MLA_Q_LORA, MLA_HEADS * (MLA_NOPE + MLA_ROPE)), MLA_Q_LORA),
        "mla_w_ukv": nrm(ks[8], (n_a, MLA_KV_LORA, MLA_HEADS * (MLA_NOPE + MLA_V)), MLA_KV_LORA),
        "mla_q_gain": gain(ks[9], (n_a, MLA_NOPE + MLA_ROPE)),
        "mla_k_gain": gain(ks[10], (n_a, MLA_NOPE + MLA_ROPE)),
        "mla_w_o": nrm(ks[11], (n_a, MLA_HEADS * MLA_V, D_MODEL), MLA_HEADS * MLA_V),
        "dil_w_qkv": nrm(ks[12], (n_b, D_MODEL, 3 * DIL_GROUPS * DIL_HEADS * DIL_HEAD_DIM), D_MODEL),
        "dil_q_gain": gain(ks[13], (n_b, DIL_GROUPS, DIL_HEAD_DIM)),
        "dil_k_gain": gain(ks[14], (n_b, DIL_GROUPS, DIL_HEAD_DIM)),
        "dil_w_o": nrm(ks[15], (n_b, DIL_HEADS * DIL_HEAD_DIM, D_MODEL), DIL_HEADS * DIL_HEAD_DIM),
        "ffn_w_gate": nrm(ks[16], (DEPTH, D_MODEL, FFN_HIDDEN), D_MODEL),
        "ffn_w_up": nrm(ks[17], (DEPTH, D_MODEL, FFN_HIDDEN), D_MODEL),
        "ffn_w_down": nrm(ks[18], (DEPTH, FFN_HIDDEN, D_MODEL), FFN_HIDDEN),
    }


def reference(x, positions, mixer_norm, ffn_norm, mla_w_down, mla_q_norm, mla_kv_norm, mla_w_uq,
              mla_w_ukv, mla_q_gain, mla_k_gain, mla_w_o, dil_w_qkv, dil_q_gain, dil_k_gain, dil_w_o,
              ffn_w_gate, ffn_w_up, ffn_w_down):
    for i in range(DEPTH):
        h = rms_norm(x, mixer_norm[i])
        j = i // 2
        if i % 2 == 0:
            mix = mla_mixer(h, positions, mla_w_down[j], mla_q_norm[j], mla_kv_norm[j], mla_w_uq[j],
                            mla_w_ukv[j], mla_q_gain[j], mla_k_gain[j], mla_w_o[j])
        else:
            mix = dilated_mixer(h, positions, dil_w_qkv[j], dil_q_gain[j], dil_k_gain[j], dil_w_o[j])
        x = x + mix
        x = x + swiglu(rms_norm(x, ffn_norm[i]), ffn_w_gate[i], ffn_w_up[i], ffn_w_down[i])
    return x
```

```python
import functools
import math

import jax
import jax.numpy as jnp
from jax import lax
from jax.experimental import pallas as pl
from jax.experimental.pallas import tpu as pltpu

F32 = jnp.float32
BF16 = jnp.bfloat16

EPS = 1e-6
ROPE_THETA = 10000.0
NEG_INF = -1e30

LANE = 128
HEAD_DIM = 128
MLA_ROPE = 64
DIL_PAIRS = ((128, 1), (512, 4), (2048, 16))
DIL_SPAN = 128
ATTN_BLK = 128

VMEM_LIMIT_BYTES = 56 * 1024 * 1024


def _params(*sem):
    return pltpu.CompilerParams(dimension_semantics=sem, vmem_limit_bytes=VMEM_LIMIT_BYTES)


def _rms_scale(x, width):
    return lax.rsqrt(jnp.sum(x * x, axis=-1, keepdims=True) * (1.0 / width) + EPS)


def _mla_proj_kernel(x_ref, pos_ref, invf_ref, sgn_ref, gmix_ref, wd_ref, qn_ref, kvn_ref,
                     wuq_ref, wukv_ref, qgn_ref, qgr_ref, kgn_ref, kgr_ref,
                     qcat_ref, kn_ref, kr_ref, v_ref, *, n_heads, q_lora, kv_lora, d_model):
    x = x_ref[...]
    h = (x * _rms_scale(x, d_model) * gmix_ref[...]).astype(BF16)
    down = jnp.dot(h, wd_ref[...], preferred_element_type=F32)
    cq = down[:, :q_lora]
    ckv = down[:, q_lora:q_lora + kv_lora]
    kr_raw = down[:, q_lora + kv_lora:]
    cq = (cq * _rms_scale(cq, q_lora) * qn_ref[...]).astype(BF16)
    ckv = (ckv * _rms_scale(ckv, kv_lora) * kvn_ref[...]).astype(BF16)
    q = jnp.dot(cq, wuq_ref[...], preferred_element_type=F32)
    kv = jnp.dot(ckv, wukv_ref[...], preferred_element_type=F32)

    ang = pos_ref[...] * invf_ref[...]
    cos = jnp.cos(ang)
    sin = jnp.sin(ang) * sgn_ref[...]
    lane = lax.broadcasted_iota(jnp.int32, cos.shape, 1)
    first_half = lane < MLA_ROPE // 2

    def rope(slab):
        partner = jnp.where(first_half, pltpu.roll(slab, LANE - MLA_ROPE // 2, 1),
                            pltpu.roll(slab, MLA_ROPE // 2, 1))
        return slab * cos + partner * sin

    qgn, qgr, kgn, kgr = qgn_ref[...], qgr_ref[...], kgn_ref[...], kgr_ref[...]
    for hd in range(n_heads):
        base = hd * 2 * LANE
        nope = q[:, base:base + LANE]
        qcat_ref[:, base:base + LANE] = (nope * _rms_scale(nope, HEAD_DIM) * qgn).astype(BF16)
        rp = q[:, base + LANE:base + 2 * LANE]
        qcat_ref[:, base + LANE:base + 2 * LANE] = rope(rp * _rms_scale(rp, MLA_ROPE) * qgr).astype(BF16)
        kk = kv[:, hd * LANE:(hd + 1) * LANE]
        kn_ref[:, hd * LANE:(hd + 1) * LANE] = (kk * _rms_scale(kk, HEAD_DIM) * kgn).astype(BF16)
    kr_ref[...] = rope(kr_raw * _rms_scale(kr_raw, MLA_ROPE) * kgr).astype(BF16)
    v_ref[...] = kv[:, n_heads * LANE:].astype(BF16)


def _mla_proj(x2d, pos, invf, sgn, gmix, wd, qn, kvn, wuq, wukv, qgn, qgr, kgn, kgr, *, n_heads, tm):
    t, d = x2d.shape
    q_lora, kv_lora = qn.shape[1], kvn.shape[1]
    hd = n_heads * LANE
    row = lambda i: (i, 0)
    const = lambda i: (0, 0)
    full = lambda a: pl.BlockSpec(a.shape, const)
    return pl.pallas_call(
        functools.partial(_mla_proj_kernel, n_heads=n_heads, q_lora=q_lora, kv_lora=kv_lora, d_model=d),
        grid=(t // tm,),
        in_specs=[pl.BlockSpec((tm, d), row), pl.BlockSpec((tm, 1), row), full(invf), full(sgn), full(gmix),
                  full(wd), full(qn), full(kvn), full(wuq), full(wukv), full(qgn), full(qgr), full(kgn),
                  full(kgr)],
        out_specs=[pl.BlockSpec((tm, 2 * hd), row), pl.BlockSpec((tm, hd), row),
                   pl.BlockSpec((tm, LANE), row), pl.BlockSpec((tm, hd), row)],
        out_shape=[jax.ShapeDtypeStruct((t, 2 * hd), BF16), jax.ShapeDtypeStruct((t, hd), BF16),
                   jax.ShapeDtypeStruct((t, LANE), BF16), jax.ShapeDtypeStruct((t, hd), BF16)],
        compiler_params=_params("parallel"),
        name="mla_proj",
    )(x2d, pos, invf, sgn, gmix, wd, qn, kvn, wuq, wukv, qgn, qgr, kgn, kgr)


def _mla_attn_kernel(q_ref, kn_ref, kr_ref, v_ref, o_ref, *, tq):
    qi = pl.program_id(2)
    q = q_ref[...]

    def scores(j):
        rows = pl.ds(pl.multiple_of(j * tq, tq), tq)
        k = jnp.concatenate([kn_ref[rows, :], kr_ref[rows, :]], axis=-1)
        s = lax.dot_general(q, k, (((1,), (1,)), ((), ())), preferred_element_type=F32)
        return s, v_ref[rows, :]

    def update(carry, s, v):
        m, l, acc = carry
        m_new = jnp.maximum(m, jnp.max(s, axis=-1, keepdims=True))
        alpha = jnp.exp(m - m_new)
        p = jnp.exp(s - m_new)
        l = alpha * l + jnp.sum(p, axis=-1, keepdims=True)
        acc = alpha * acc + jnp.dot(p.astype(BF16), v, preferred_element_type=F32)
        return m_new, l, acc

    def body(j, carry):
        s, v = scores(j)
        return update(carry, s, v)

    init = (jnp.full((tq, 1), NEG_INF, F32), jnp.zeros((tq, 1), F32), jnp.zeros((tq, HEAD_DIM), F32))
    carry = lax.fori_loop(0, qi, body, init)
    s, v = scores(qi)
    r = lax.broadcasted_iota(jnp.int32, (tq, tq), 0)
    c = lax.broadcasted_iota(jnp.int32, (tq, tq), 1)
    _, l, acc = update(carry, jnp.where(c <= r, s, NEG_INF), v)
    o_ref[...] = (acc / l).astype(BF16)


def _mla_attn(qcat, kn, kr, v, *, n_heads, tq):
    b, s, _ = kn.shape
    return pl.pallas_call(
        functools.partial(_mla_attn_kernel, tq=tq),
        grid=(b, n_heads, s // tq),
        in_specs=[pl.BlockSpec((None, tq, 2 * LANE), lambda bi, hi, qi: (bi, qi, hi)),
                  pl.BlockSpec((None, s, LANE), lambda bi, hi, qi: (bi, 0, hi)),
                  pl.BlockSpec((None, s, LANE), lambda bi, hi, qi: (bi, 0, 0)),
                  pl.BlockSpec((None, s, LANE), lambda bi, hi, qi: (bi, 0, hi))],
        out_specs=pl.BlockSpec((None, tq, LANE), lambda bi, hi, qi: (bi, qi, hi)),
        out_shape=jax.ShapeDtypeStruct((b, s, n_heads * LANE), BF16),
        compiler_params=_params("parallel", "parallel", "arbitrary"),
        name="mla_attn",
    )(qcat, kn, kr, v)


def _proj_resid_kernel(a_ref, w_ref, r_ref, o_ref):
    o_ref[...] = r_ref[...] + jnp.dot(a_ref[...], w_ref[...], preferred_element_type=F32)


def _proj_resid(a, w, resid, *, tm):
    t, k = a.shape
    n = w.shape[1]
    return pl.pallas_call(
        _proj_resid_kernel,
        grid=(t // tm,),
        in_specs=[pl.BlockSpec((tm, k), lambda i: (i, 0)), pl.BlockSpec((k, n), lambda i: (0, 0)),
                  pl.BlockSpec((tm, n), lambda i: (i, 0))],
        out_specs=pl.BlockSpec((tm, n), lambda i: (i, 0)),
        out_shape=jax.ShapeDtypeStruct((t, n), F32),
        compiler_params=_params("parallel"),
        name="proj_resid",
    )(a, w, resid)


def _ffn_kernel(x_ref, g_ref, wg_ref, wu_ref, wd_ref, o_ref, h_ref, *, d_model):
    @pl.when(pl.program_id(1) == 0)
    def _():
        x = x_ref[...]
        h_ref[...] = (x * _rms_scale(x, d_model) * g_ref[...]).astype(BF16)
        o_ref[...] = x

    h = h_ref[...]
    gate = jnp.dot(h, wg_ref[...], preferred_element_type=F32)
    up = jnp.dot(h, wu_ref[...], preferred_element_type=F32)
    a = (gate * jax.nn.sigmoid(gate) * up).astype(BF16)
    o_ref[...] += jnp.dot(a, wd_ref[...], preferred_element_type=F32)


def _ffn(x2d, gain, wg, wu, wd, *, tm, tf):
    t, d = x2d.shape
    f = wg.shape[1]
    return pl.pallas_call(
        functools.partial(_ffn_kernel, d_model=d),
        grid=(t // tm, f // tf),
        in_specs=[pl.BlockSpec((tm, d), lambda i, j: (i, 0)), pl.BlockSpec((1, d), lambda i, j: (0, 0)),
                  pl.BlockSpec((d, tf), lambda i, j: (0, j)), pl.BlockSpec((d, tf), lambda i, j: (0, j)),
                  pl.BlockSpec((tf, d), lambda i, j: (j, 0))],
        out_specs=pl.BlockSpec((tm, d), lambda i, j: (i, 0)),
        out_shape=jax.ShapeDtypeStruct((t, d), F32),
        scratch_shapes=[pltpu.VMEM((tm, d), BF16)],
        compiler_params=_params("parallel", "arbitrary"),
        name="ffn",
    )(x2d, gain, wg, wu, wd)


def _dil_qkv_kernel(x_ref, pos_ref, invf_ref, sgn_ref, gmix_ref, w_ref, gains_ref, o_ref,
                    h_ref, cos_ref, sin_ref, *, tn, n_qk_tiles, tiles_per_region, d_model):
    j = pl.program_id(1)

    @pl.when(j == 0)
    def _():
        x = x_ref[...]
        h_ref[...] = (x * _rms_scale(x, d_model) * gmix_ref[...]).astype(BF16)
        ang = pos_ref[...] * invf_ref[...]
        cos_ref[...] = jnp.cos(ang)
        sin_ref[...] = jnp.sin(ang) * sgn_ref[...]

    y = jnp.dot(h_ref[...], w_ref[...], preferred_element_type=F32)

    @pl.when(j < n_qk_tiles)
    def _():
        gain = gains_ref[pl.ds(j // tiles_per_region, 1), :]
        cos, sin = cos_ref[...], sin_ref[...]
        for hd in range(tn // LANE):
            cs = slice(hd * LANE, (hd + 1) * LANE)
            seg = y[:, cs]
            seg = seg * _rms_scale(seg, HEAD_DIM) * gain
            o_ref[:, cs] = (seg * cos + pltpu.roll(seg, HEAD_DIM // 2, 1) * sin).astype(BF16)

    @pl.when(j >= n_qk_tiles)
    def _():
        o_ref[...] = y.astype(BF16)


def _dil_qkv(x2d, pos, invf, sgn, gmix, w, gains, *, tm, tn, region):
    t, d = x2d.shape
    n = w.shape[1]
    n_qk_tiles = (2 * n // 3) // tn
    return pl.pallas_call(
        functools.partial(_dil_qkv_kernel, tn=tn, n_qk_tiles=n_qk_tiles, tiles_per_region=region // tn,
                          d_model=d),
        grid=(t // tm, n // tn),
        in_specs=[pl.BlockSpec((tm, d), lambda i, j: (i, 0)), pl.BlockSpec((tm, 1), lambda i, j: (i, 0)),
                  pl.BlockSpec(invf.shape, lambda i, j: (0, 0)), pl.BlockSpec(sgn.shape, lambda i, j: (0, 0)),
                  pl.BlockSpec(gmix.shape, lambda i, j: (0, 0)), pl.BlockSpec((d, tn), lambda i, j: (0, j)),
                  pl.BlockSpec(gains.shape, lambda i, j: (0, 0))],
        out_specs=pl.BlockSpec((tm, tn), lambda i, j: (i, j)),
        out_shape=jax.ShapeDtypeStruct((t, n), BF16),
        scratch_shapes=[pltpu.VMEM((tm, d), BF16), pltpu.VMEM((tm, LANE), F32), pltpu.VMEM((tm, LANE), F32)],
        compiler_params=_params("parallel", "arbitrary"),
        name="dil_qkv",
    )(x2d, pos, invf, sgn, gmix, w, gains)


def _dil_attn_kernel(*refs, lb, n_heads, has_prev):
    if has_prev:
        q_ref, k_ref, v_ref, kp_ref, vp_ref, o_ref, lse_ref = refs
    else:
        q_ref, k_ref, v_ref, o_ref, lse_ref = refs
    blk = ATTN_BLK
    not_first = pl.program_id(2) > 0
    ii = lax.broadcasted_iota(jnp.int32, (blk, 2 * blk), 0)
    jj = lax.broadcasted_iota(jnp.int32, (blk, 2 * blk), 1)
    band = (jj >= ii) & (jj <= ii + DIL_SPAN)
    bias_band = jnp.where(band, 0.0, NEG_INF).astype(F32)
    bias_head = jnp.where(band & (jj >= blk), 0.0, NEG_INF).astype(F32)
    bias_tri = bias_head[:, blk:]
    lane = lax.broadcasted_iota(jnp.int32, (blk, LANE), 1)
    contract_last = (((1,), (1,)), ((), ()))

    for i in range(lb // blk):
        rows = slice(i * blk, (i + 1) * blk)
        lse_tile = jnp.zeros((blk, LANE), F32)
        for hd in range(n_heads):
            cs = slice(hd * LANE, (hd + 1) * LANE)
            q = q_ref[rows, cs]
            if i > 0:
                k = k_ref[(i - 1) * blk:(i + 1) * blk, cs]
                v = v_ref[(i - 1) * blk:(i + 1) * blk, cs]
                bias = bias_band
            elif has_prev:
                k = jnp.concatenate([kp_ref[:, cs], k_ref[rows, cs]], axis=0)
                v = jnp.concatenate([vp_ref[:, cs], v_ref[rows, cs]], axis=0)
                bias = jnp.where(not_first, bias_band, bias_head)
            else:
                k = k_ref[rows, cs]
                v = v_ref[rows, cs]
                bias = bias_tri
            s = lax.dot_general(q, k, contract_last, preferred_element_type=F32) + bias
            m = jnp.max(s, axis=-1, keepdims=True)
            p = jnp.exp(s - m)
            l = jnp.sum(p, axis=-1, keepdims=True)
            o = jnp.dot(p.astype(BF16), v, preferred_element_type=F32) / l
            o_ref[rows, cs] = o.astype(BF16)
            lse_tile = jnp.where(lane == hd, m + jnp.log(l), lse_tile)
        lse_ref[rows, :] = lse_tile


def _dil_attn(qkv, *, group, n_groups, dilation, n_heads, seq):
    t, n = qkv.shape
    b = t // seq
    l = seq // dilation
    lb = min(l, 2 * ATTN_BLK)
    nsub = lb // ATTN_BLK
    has_prev = l > lb
    hd = n_heads * LANE
    qkv3 = qkv.reshape(b, l, dilation * n)
    col = lambda ri, c: ri * (n // hd) + c * n_groups + group
    cur = lambda c: pl.BlockSpec((None, lb, hd), lambda bi, ri, li: (bi, li, col(ri, c)))
    prev = lambda c: pl.BlockSpec(
        (None, ATTN_BLK, hd), lambda bi, ri, li: (bi, jnp.maximum(li * nsub - 1, 0), col(ri, c)))
    in_specs = [cur(0), cur(1), cur(2)]
    args = [qkv3, qkv3, qkv3]
    if has_prev:
        in_specs += [prev(1), prev(2)]
        args += [qkv3, qkv3]
    o, lse = pl.pallas_call(
        functools.partial(_dil_attn_kernel, lb=lb, n_heads=n_heads, has_prev=has_prev),
        grid=(b, dilation, l // lb),
        in_specs=in_specs,
        out_specs=[pl.BlockSpec((None, lb, hd), lambda bi, ri, li: (bi, li, ri)),
                   pl.BlockSpec((None, lb, LANE), lambda bi, ri, li: (bi, li, ri))],
        out_shape=[jax.ShapeDtypeStruct((b, l, dilation * hd), BF16),
                   jax.ShapeDtypeStruct((b, l, dilation * LANE), F32)],
        compiler_params=_params("parallel", "parallel", "arbitrary"),
        name=f"dil_attn_g{group}",
    )(*args)
    return o.reshape(t, hd), lse.reshape(t, LANE)


def _dil_out_kernel(o0_ref, o1_ref, o2_ref, l0_ref, l1_ref, l2_ref, x_ref, w_ref, out_ref, oc_ref, *, n_heads):
    l0, l1, l2 = l0_ref[...], l1_ref[...], l2_ref[...]
    m = jnp.maximum(jnp.maximum(l0, l1), l2)
    e0, e1, e2 = jnp.exp(l0 - m), jnp.exp(l1 - m), jnp.exp(l2 - m)
    inv = 1.0 / (e0 + e1 + e2)
    w0, w1, w2 = e0 * inv, e1 * inv, e2 * inv
    for hd in range(n_heads):
        cs = slice(hd * LANE, (hd + 1) * LANE)
        hs = slice(hd, hd + 1)
        oc = (w0[:, hs] * o0_ref[:, cs].astype(F32) + w1[:, hs] * o1_ref[:, cs].astype(F32)
              + w2[:, hs] * o2_ref[:, cs].astype(F32))
        oc_ref[:, cs] = oc.astype(BF16)
    out_ref[...] = x_ref[...] + jnp.dot(oc_ref[...], w_ref[...], preferred_element_type=F32)


def _dil_out(os, lses, x2d, w, *, n_heads, tm):
    t, d = x2d.shape
    hd = n_heads * LANE
    row = lambda i: (i, 0)
    return pl.pallas_call(
        functools.partial(_dil_out_kernel, n_heads=n_heads),
        grid=(t // tm,),
        in_specs=[pl.BlockSpec((tm, hd), row)] * 3 + [pl.BlockSpec((tm, LANE), row)] * 3
        + [pl.BlockSpec((tm, d), row), pl.BlockSpec(w.shape, lambda i: (0, 0))],
        out_specs=pl.BlockSpec((tm, d), row),
        out_shape=jax.ShapeDtypeStruct((t, d), F32),
        scratch_shapes=[pltpu.VMEM((tm, hd), BF16)],
        compiler_params=_params("parallel"),
        name="dil_out",
    )(*os, *lses, x2d, w)


def _rope_rows(dim, live):
    half = dim // 2
    inv_freq = jnp.power(ROPE_THETA, -2.0 * jnp.arange(half, dtype=F32) / dim)
    lane = jnp.arange(LANE)
    invf = jnp.where(lane < live, inv_freq[lane % half], 0.0).astype(F32)
    sgn = jnp.where(lane < live, jnp.where(lane % dim < half, -1.0, 1.0), 0.0).astype(F32)
    return invf[None, :], sgn[None, :]


def _pad_lanes(v):
    return jnp.pad(v, (0, LANE - v.shape[0]))[None, :]


def kernel(x, positions, mixer_norm, ffn_norm, mla_w_down, mla_q_norm, mla_kv_norm, mla_w_uq, mla_w_ukv,
           mla_q_gain, mla_k_gain, mla_w_o, dil_w_qkv, dil_q_gain, dil_k_gain, dil_w_o, ffn_w_gate,
           ffn_w_up, ffn_w_down):
    b, s, d = x.shape
    t = b * s
    depth = mixer_norm.shape[0]
    n_heads = d // HEAD_DIM
    x2d = x.reshape(t, d)
    pos = positions.astype(F32).reshape(t, 1)

    for i in range(depth):
        j = i // 2
        gmix = mixer_norm[i][None, :]
        if i % 2 == 0:
            q_lora = mla_q_norm.shape[1]
            kv_lora = mla_kv_norm.shape[1]
            scale = 1.0 / math.sqrt(HEAD_DIM + MLA_ROPE)
            invf, sgn = _rope_rows(MLA_ROPE, MLA_ROPE)
            wd = jnp.pad(mla_w_down[j], ((0, 0), (0, LANE - MLA_ROPE))).astype(BF16)
            wuq = mla_w_uq[j].reshape(q_lora, n_heads, HEAD_DIM + MLA_ROPE)
            wuq = jnp.pad(wuq, ((0, 0), (0, 0), (0, LANE - MLA_ROPE))).reshape(q_lora, n_heads * 2 * LANE)
            wukv = mla_w_ukv[j].reshape(kv_lora, n_heads, 2 * HEAD_DIM)
            wukv = jnp.concatenate([wukv[:, :, :HEAD_DIM].reshape(kv_lora, n_heads * LANE),
                                    wukv[:, :, HEAD_DIM:].reshape(kv_lora, n_heads * LANE)], axis=1)
            qg = mla_q_gain[j] * scale
            kg = mla_k_gain[j]
            qcat, kn, kr, v = _mla_proj(
                x2d, pos, invf, sgn, gmix, wd, mla_q_norm[j][None, :], mla_kv_norm[j][None, :],
                wuq.astype(BF16), wukv.astype(BF16), qg[None, :HEAD_DIM], _pad_lanes(qg[HEAD_DIM:]),
                kg[None, :HEAD_DIM], _pad_lanes(kg[HEAD_DIM:]), n_heads=n_heads, tm=256)
            o = _mla_attn(qcat.reshape(b, s, -1), kn.reshape(b, s, -1), kr.reshape(b, s, -1),
                          v.reshape(b, s, -1), n_heads=n_heads, tq=256)
            x2d = _proj_resid(o.reshape(t, -1), mla_w_o[j].astype(BF16), x2d, tm=512)
        else:
            n_groups = len(DIL_PAIRS)
            scale = 1.0 / math.sqrt(HEAD_DIM)
            invf, sgn = _rope_rows(HEAD_DIM, HEAD_DIM)
            gains = jnp.concatenate([dil_q_gain[j] * scale, dil_k_gain[j]], axis=0)
            qkv = _dil_qkv(x2d, pos, invf, sgn, gmix, dil_w_qkv[j].astype(BF16), gains,
                           tm=min(t, 1024), tn=min(1024, n_heads * LANE), region=n_heads * LANE)
            os, lses = [], []
            for g, (_, dilation) in enumerate(DIL_PAIRS):
                o_g, lse_g = _dil_attn(qkv, group=g, n_groups=n_groups, dilation=dilation,
                                       n_heads=n_heads, seq=s)
                os.append(o_g)
                lses.append(lse_g)
            x2d = _dil_out(os, lses, x2d, dil_w_o[j].astype(BF16), n_heads=n_heads, tm=256)
        x2d = _ffn(x2d, ffn_norm[i][None, :], ffn_w_gate[i].astype(BF16), ffn_w_up[i].astype(BF16),
                   ffn_w_down[i].astype(BF16), tm=512, tf=512)
    return x2d.reshape(b, s, d)
```

```python
import functools
import math

import jax
import jax.numpy as jnp
from jax import lax
from jax.experimental import pallas as pl
from jax.experimental.pallas import tpu as pltpu

F32 = jnp.float32
BF16 = jnp.bfloat16

EPS = 1e-6
ROPE_THETA = 10000.0
NEG_INF = -1e30

LANE = 128
MXU_DIM = 256
HEAD_DIM = 128
MLA_ROPE = 64
DIL_PAIRS = ((128, 1), (512, 4), (2048, 16))
DIL_SPAN = 128
ATTN_BLK = 128

VMEM_LIMIT_BYTES = 56 * 1024 * 1024


def _params(*sem):
    return pltpu.CompilerParams(dimension_semantics=sem, vmem_limit_bytes=VMEM_LIMIT_BYTES)


def _rms_scale(x, width):
    return lax.rsqrt(jnp.sum(x * x, axis=-1, keepdims=True) * (1.0 / width) + EPS)


def _mla_proj_kernel(x_ref, pos_ref, invf_ref, sgn_ref, gmix_ref, wd_ref, qn_ref, kvn_ref,
                     wuq_ref, wukv_ref, qgn_ref, qgr_ref, kgn_ref, kgr_ref,
                     qcat_ref, kn_ref, kr_ref, v_ref, *, n_heads, q_lora, kv_lora, d_model):
    x = x_ref[...]
    h = (x * _rms_scale(x, d_model) * gmix_ref[...]).astype(BF16)
    down = jnp.dot(h, wd_ref[...], preferred_element_type=F32)
    cq = down[:, :q_lora]
    ckv = down[:, q_lora:q_lora + kv_lora]
    kr_raw = down[:, q_lora + kv_lora:]
    cq = (cq * _rms_scale(cq, q_lora) * qn_ref[...]).astype(BF16)
    ckv = (ckv * _rms_scale(ckv, kv_lora) * kvn_ref[...]).astype(BF16)
    q = jnp.dot(cq, wuq_ref[...], preferred_element_type=F32)
    kv = jnp.dot(ckv, wukv_ref[...], preferred_element_type=F32)

    ang = pos_ref[...] * invf_ref[...]
    cos = jnp.cos(ang)
    sin = jnp.sin(ang) * sgn_ref[...]
    lane = lax.broadcasted_iota(jnp.int32, cos.shape, 1)
    first_half = lane < MLA_ROPE // 2

    def rope(slab):
        partner = jnp.where(first_half, pltpu.roll(slab, LANE - MLA_ROPE // 2, 1),
                            pltpu.roll(slab, MLA_ROPE // 2, 1))
        return slab * cos + partner * sin

    qgn, qgr, kgn, kgr = qgn_ref[...], qgr_ref[...], kgn_ref[...], kgr_ref[...]
    for hd in range(n_heads):
        base = hd * 2 * LANE
        nope = q[:, base:base + LANE]
        qcat_ref[:, base:base + LANE] = (nope * _rms_scale(nope, HEAD_DIM) * qgn).astype(BF16)
        rp = q[:, base + LANE:base + 2 * LANE]
        qcat_ref[:, base + LANE:base + 2 * LANE] = rope(rp * _rms_scale(rp, MLA_ROPE) * qgr).astype(BF16)
        kk = kv[:, hd * LANE:(hd + 1) * LANE]
        kn_ref[:, hd * LANE:(hd + 1) * LANE] = (kk * _rms_scale(kk, HEAD_DIM) * kgn).astype(BF16)
    kr_ref[...] = rope(kr_raw * _rms_scale(kr_raw, MLA_ROPE) * kgr).astype(BF16)
    v_ref[...] = kv[:, n_heads * LANE:].astype(BF16)


def _mla_proj(x2d, pos, invf, sgn, gmix, wd, qn, kvn, wuq, wukv, qgn, qgr, kgn, kgr, *, n_heads, tm):
    t, d = x2d.shape
    q_lora, kv_lora = qn.shape[1], kvn.shape[1]
    hd = n_heads * LANE
    row = lambda i: (i, 0)
    const = lambda i: (0, 0)
    full = lambda a: pl.BlockSpec(a.shape, const)
    return pl.pallas_call(
        functools.partial(_mla_proj_kernel, n_heads=n_heads, q_lora=q_lora, kv_lora=kv_lora, d_model=d),
        grid=(t // tm,),
        in_specs=[pl.BlockSpec((tm, d), row), pl.BlockSpec((tm, 1), row), full(invf), full(sgn), full(gmix),
                  full(wd), full(qn), full(kvn), full(wuq), full(wukv), full(qgn), full(qgr), full(kgn),
                  full(kgr)],
        out_specs=[pl.BlockSpec((tm, 2 * hd), row), pl.BlockSpec((tm, hd), row),
                   pl.BlockSpec((tm, LANE), row), pl.BlockSpec((tm, hd), row)],
        out_shape=[jax.ShapeDtypeStruct((t, 2 * hd), BF16), jax.ShapeDtypeStruct((t, hd), BF16),
                   jax.ShapeDtypeStruct((t, LANE), BF16), jax.ShapeDtypeStruct((t, hd), BF16)],
        compiler_params=_params("parallel"),
        name="mla_proj",
    )(x2d, pos, invf, sgn, gmix, wd, qn, kvn, wuq, wukv, qgn, qgr, kgn, kgr)


def _mla_attn_kernel(q_ref, kn_ref, kr_ref, v_ref, o_ref, s_ref, p_ref, *, tq, tk):
    seq = q_ref.shape[0]
    v_t = v_ref[...].astype(F32).T.astype(BF16)
    contract_last = (((1,), (1,)), ((), ()))
    for qi in range(seq // tq):
        q = q_ref[qi * tq:(qi + 1) * tq, :]
        n_chunks = (qi + 1) * tq // tk
        m = None
        for j in range(n_chunks):
            rows = slice(j * tk, (j + 1) * tk)
            k = jnp.concatenate([kn_ref[rows, :], kr_ref[rows, :]], axis=-1)
            s = lax.dot_general(k, q, contract_last, preferred_element_type=F32)
            if (j + 1) * tk > qi * tq:
                key = j * tk + lax.broadcasted_iota(jnp.int32, (tk, tq), 0)
                qry = qi * tq + lax.broadcasted_iota(jnp.int32, (tk, tq), 1)
                s = jnp.where(key <= qry, s, NEG_INF)
            s_ref[rows, :] = s
            mj = jnp.max(s, axis=0, keepdims=True)
            m = mj if m is None else jnp.maximum(m, mj)
        l = jnp.zeros((1, tq), F32)
        for j in range(n_chunks):
            rows = slice(j * tk, (j + 1) * tk)
            p = jnp.exp(s_ref[rows, :] - m)
            l = l + jnp.sum(p, axis=0, keepdims=True)
            p_ref[rows, :] = p.astype(BF16)
        kv = n_chunks * tk
        o_t = jnp.dot(v_t[:, :kv], p_ref[0:kv, :], preferred_element_type=F32)
        o_ref[qi * tq:(qi + 1) * tq, :] = (o_t / l).T.astype(BF16)


def _mla_attn(qcat, kn, kr, v, *, n_heads, tq, tk):
    b, s, _ = kn.shape
    head = lambda bi, hi: (bi, 0, hi)
    return pl.pallas_call(
        functools.partial(_mla_attn_kernel, tq=tq, tk=tk),
        grid=(b, n_heads),
        in_specs=[pl.BlockSpec((None, s, 2 * LANE), head), pl.BlockSpec((None, s, LANE), head),
                  pl.BlockSpec((None, s, LANE), lambda bi, hi: (bi, 0, 0)),
                  pl.BlockSpec((None, s, LANE), head)],
        out_specs=pl.BlockSpec((None, s, LANE), head),
        out_shape=jax.ShapeDtypeStruct((b, s, n_heads * LANE), BF16),
        scratch_shapes=[pltpu.VMEM((s, tq), F32), pltpu.VMEM((s, tq), BF16)],
        compiler_params=_params("parallel", "parallel"),
        name="mla_attn",
    )(qcat, kn, kr, v)


def _proj_resid_kernel(a_ref, w_ref, r_ref, o_ref):
    o_ref[...] = r_ref[...] + jnp.dot(a_ref[...], w_ref[...], preferred_element_type=F32)


def _proj_resid(a, w, resid, *, tm):
    t, k = a.shape
    n = w.shape[1]
    return pl.pallas_call(
        _proj_resid_kernel,
        grid=(t // tm,),
        in_specs=[pl.BlockSpec((tm, k), lambda i: (i, 0)), pl.BlockSpec((k, n), lambda i: (0, 0)),
                  pl.BlockSpec((tm, n), lambda i: (i, 0))],
        out_specs=pl.BlockSpec((tm, n), lambda i: (i, 0)),
        out_shape=jax.ShapeDtypeStruct((t, n), F32),
        compiler_params=_params("parallel"),
        name="proj_resid",
    )(a, w, resid)


def _ffn_kernel(x_ref, g_ref, wg_ref, wu_ref, wd_ref, o_ref, h_ref, *, d_model):
    @pl.when(pl.program_id(1) == 0)
    def _():
        x = x_ref[...]
        h_ref[...] = (x * _rms_scale(x, d_model) * g_ref[...]).astype(BF16)
        o_ref[...] = x

    h = h_ref[...]
    gate = jnp.dot(h, wg_ref[...], preferred_element_type=F32)
    up = jnp.dot(h, wu_ref[...], preferred_element_type=F32)
    a = (gate * jax.nn.sigmoid(gate) * up).astype(BF16)
    o_ref[...] += jnp.dot(a, wd_ref[...], preferred_element_type=F32)


def _ffn(x2d, gain, wg, wu, wd, *, tm, tf):
    t, d = x2d.shape
    f = wg.shape[1]
    return pl.pallas_call(
        functools.partial(_ffn_kernel, d_model=d),
        grid=(t // tm, f // tf),
        in_specs=[pl.BlockSpec((tm, d), lambda i, j: (i, 0)), pl.BlockSpec((1, d), lambda i, j: (0, 0)),
                  pl.BlockSpec((d, tf), lambda i, j: (0, j)), pl.BlockSpec((d, tf), lambda i, j: (0, j)),
                  pl.BlockSpec((tf, d), lambda i, j: (j, 0))],
        out_specs=pl.BlockSpec((tm, d), lambda i, j: (i, 0)),
        out_shape=jax.ShapeDtypeStruct((t, d), F32),
        scratch_shapes=[pltpu.VMEM((tm, d), BF16)],
        compiler_params=_params("parallel", "arbitrary"),
        name="ffn",
    )(x2d, gain, wg, wu, wd)


def _dil_prep_kernel(x_ref, pos_ref, invf_ref, sgn_ref, g_ref, *refs, dilations, d_model):
    n_out = 3 * len(dilations)
    outs, (h_s, cos_s, sin_s) = refs[:n_out], refs[n_out:]
    tm = x_ref.shape[0]
    n_tiles = d_model // LANE
    x = x_ref[...]
    h = x * _rms_scale(x, d_model) * g_ref[...]
    for c in range(n_tiles):
        h_s[c] = h[:, c * LANE:(c + 1) * LANE]
    ang = pos_ref[...] * invf_ref[...]
    cos_s[...] = jnp.cos(ang)
    sin_s[...] = jnp.sin(ang) * sgn_ref[...]
    for gi, d in enumerate(dilations):
        h_o, cos_o, sin_o = outs[3 * gi:3 * gi + 3]
        if d == 1:
            h_o[...] = h.astype(BF16)
            cos_o[...] = cos_s[...]
            sin_o[...] = sin_s[...]
        else:
            for r in range(d):
                rows = pl.ds(r, tm // d, stride=d)
                for c in range(n_tiles):
                    h_o[r, :, c * LANE:(c + 1) * LANE] = h_s[c, rows, :].astype(BF16)
                cos_o[r] = cos_s[rows, :]
                sin_o[r] = sin_s[rows, :]


def _dil_prep(x2d, pos, invf, sgn, gain, *, dilations, seq, tm):
    t, d_model = x2d.shape
    b = t // seq
    nbl = seq // tm
    row = lambda i: (i, 0)
    const = lambda i: (0, 0)
    out_specs, out_shape = [], []
    for d in dilations:
        for width, dtype in ((d_model, BF16), (LANE, F32), (LANE, F32)):
            if d == 1:
                out_specs.append(pl.BlockSpec((tm, width), row))
                out_shape.append(jax.ShapeDtypeStruct((t, width), dtype))
            else:
                out_specs.append(pl.BlockSpec((None, d, tm // d, width), lambda i: (i // nbl, 0, i % nbl, 0)))
                out_shape.append(jax.ShapeDtypeStruct((b, d, seq // d, width), dtype))
    outs = pl.pallas_call(
        functools.partial(_dil_prep_kernel, dilations=dilations, d_model=d_model),
        grid=(t // tm,),
        in_specs=[pl.BlockSpec((tm, d_model), row), pl.BlockSpec((tm, 1), row), pl.BlockSpec(invf.shape, const),
                  pl.BlockSpec(sgn.shape, const), pl.BlockSpec(gain.shape, const)],
        out_specs=out_specs,
        out_shape=out_shape,
        scratch_shapes=[pltpu.VMEM((d_model // LANE, tm, LANE), F32), pltpu.VMEM((tm, LANE), F32),
                        pltpu.VMEM((tm, LANE), F32)],
        compiler_params=_params("parallel"),
        name="dil_prep",
    )(x2d, pos, invf, sgn, gain)
    outs = [o.reshape(t, o.shape[-1]) for o in outs]
    return [tuple(outs[3 * gi:3 * gi + 3]) for gi in range(len(dilations))]


def _dil_qkv_kernel(h_ref, cos_ref, sin_ref, w_ref, gains_ref, seg_ref, o_ref, *, tn, tiles_per_region,
                    group, n_groups):
    j = pl.program_id(1)

    @pl.when(j < 2 * tiles_per_region)
    def _():
        region = (j // tiles_per_region) * n_groups + group
        g_lo = gains_ref[pl.ds(2 * region, 1), :]
        g_hi = gains_ref[pl.ds(2 * region + 1, 1), :]
        cos, sin = cos_ref[...], sin_ref[...]
        a_cos, b_sin, b_cos, a_sin = g_lo * cos, g_hi * sin, g_hi * cos, g_lo * sin
        y = jnp.dot(h_ref[...], w_ref[...], preferred_element_type=F32)
        for c in range(0, tn, 2 * MXU_DIM):
            slabs = [y[:, c + k * LANE:c + (k + 1) * LANE] for k in range(4)]
            sq = jnp.concatenate([slabs[0] * slabs[0] + slabs[1] * slabs[1],
                                  slabs[2] * slabs[2] + slabs[3] * slabs[3]], axis=-1)
            ss = jnp.dot(sq.astype(BF16), seg_ref[...], preferred_element_type=F32)
            r = lax.rsqrt(ss * (1.0 / HEAD_DIM) + EPS)
            for p in range(2):
                a, b = slabs[2 * p], slabs[2 * p + 1]
                rp = r[:, p * LANE:(p + 1) * LANE]
                base = c + p * MXU_DIM
                o_ref[:, base:base + LANE] = ((a * a_cos - b * b_sin) * rp).astype(BF16)
                o_ref[:, base + LANE:base + 2 * LANE] = ((b * b_cos + a * a_sin) * rp).astype(BF16)

    @pl.when(j >= 2 * tiles_per_region)
    def _():
        o_ref[...] = jnp.dot(h_ref[...], w_ref[...], preferred_element_type=F32).astype(BF16)


def _dil_qkv(h, cos, sin, w, gains, seg_ones, *, group, n_groups, region, tm, tn):
    t, d = h.shape
    tpr = region // tn
    wcol = lambda i, j: (0, ((j // tpr) * n_groups + group) * tpr + j % tpr)
    return pl.pallas_call(
        functools.partial(_dil_qkv_kernel, tn=tn, tiles_per_region=tpr, group=group, n_groups=n_groups),
        grid=(t // tm, 3 * tpr),
        in_specs=[pl.BlockSpec((tm, d), lambda i, j: (i, 0)), pl.BlockSpec((tm, LANE), lambda i, j: (i, 0)),
                  pl.BlockSpec((tm, LANE), lambda i, j: (i, 0)), pl.BlockSpec((d, tn), wcol),
                  pl.BlockSpec(gains.shape, lambda i, j: (0, 0)),
                  pl.BlockSpec(seg_ones.shape, lambda i, j: (0, 0))],
        out_specs=pl.BlockSpec((tm, tn), lambda i, j: (i, j)),
        out_shape=jax.ShapeDtypeStruct((t, 3 * region), BF16),
        compiler_params=_params("parallel", "arbitrary"),
        name=f"dil_qkv_g{group}",
    )(h, cos, sin, w, gains, seg_ones)


def _dil_attn_kernel(*refs, lb, n_heads, has_prev):
    if has_prev:
        q_ref, k_ref, v_ref, kp_ref, vp_ref, o_ref, lse_ref = refs
    else:
        q_ref, k_ref, v_ref, o_ref, lse_ref = refs
    blk = ATTN_BLK
    not_first = pl.program_id(1) > 0
    ii = lax.broadcasted_iota(jnp.int32, (blk, 2 * blk), 0)
    jj = lax.broadcasted_iota(jnp.int32, (blk, 2 * blk), 1)
    band = (jj >= ii) & (jj <= ii + DIL_SPAN)
    bias_band = jnp.where(band, 0.0, NEG_INF).astype(F32)
    bias_head = jnp.where(band & (jj >= blk), 0.0, NEG_INF).astype(F32)
    bias_tri = bias_head[:, blk:]
    lane = lax.broadcasted_iota(jnp.int32, (blk, LANE), 1)
    contract_last = (((1,), (1,)), ((), ()))

    pair_lane = lax.broadcasted_iota(jnp.int32, (blk, 2 * LANE), 1) % LANE
    lo_mask = jnp.where(pair_lane < HEAD_DIM // 2, 1.0, 0.0).astype(BF16)
    pair_mask = jnp.concatenate([lo_mask, 1.0 - lo_mask], axis=0)
    stack2 = lambda a: jnp.concatenate([a, a], axis=0)
    bias_band2, bias_head2, bias_tri2 = stack2(bias_band), stack2(bias_head), stack2(bias_tri)

    for i in range(lb // blk):
        rows = slice(i * blk, (i + 1) * blk)
        krows = slice((i - 1) * blk, (i + 1) * blk)
        lse_tile = jnp.zeros((blk, LANE), F32)
        for pr in range(n_heads // 2):
            ps = slice(pr * 2 * LANE, (pr + 1) * 2 * LANE)
            q2 = stack2(q_ref[rows, ps]) * pair_mask
            if i > 0:
                k = k_ref[krows, ps]
                bias = bias_band2
            elif has_prev:
                k = jnp.concatenate([kp_ref[:, ps], k_ref[rows, ps]], axis=0)
                bias = jnp.where(not_first, bias_band2, bias_head2)
            else:
                k = k_ref[rows, ps]
                bias = bias_tri2
            s = lax.dot_general(q2, k, contract_last, preferred_element_type=F32) + bias
            m = jnp.max(s, axis=-1, keepdims=True)
            p = jnp.exp(s - m)
            l = jnp.sum(p, axis=-1, keepdims=True)
            lse = m + jnp.log(l)
            p = p.astype(BF16)
            for sub in range(2):
                hd = 2 * pr + sub
                cs = slice(hd * LANE, (hd + 1) * LANE)
                hr = slice(sub * blk, (sub + 1) * blk)
                if i > 0:
                    v = v_ref[krows, cs]
                elif has_prev:
                    v = jnp.concatenate([vp_ref[:, cs], v_ref[rows, cs]], axis=0)
                else:
                    v = v_ref[rows, cs]
                o = jnp.dot(p[hr], v, preferred_element_type=F32) / l[hr]
                o_ref[rows, cs] = o.astype(BF16)
                lse_tile = jnp.where(lane == hd, lse[hr], lse_tile)
        lse_ref[rows, :] = lse_tile


def _dil_attn(qkv, *, group, dilation, n_heads, seq):
    t, n = qkv.shape
    hd = n_heads * LANE
    l = seq // dilation
    lb = min(l, 2 * ATTN_BLK)
    nsub = lb // ATTN_BLK
    nl = l // lb
    has_prev = l > lb
    cur = lambda c: pl.BlockSpec((lb, hd), lambda ni, li: (ni * nl + li, c))
    prev = lambda c: pl.BlockSpec(
        (ATTN_BLK, hd), lambda ni, li: (ni * nl * nsub + jnp.maximum(li * nsub - 1, 0), c))
    in_specs = [cur(0), cur(1), cur(2)]
    args = [qkv, qkv, qkv]
    if has_prev:
        in_specs += [prev(1), prev(2)]
        args += [qkv, qkv]
    return pl.pallas_call(
        functools.partial(_dil_attn_kernel, lb=lb, n_heads=n_heads, has_prev=has_prev),
        grid=(t // l, nl),
        in_specs=in_specs,
        out_specs=[pl.BlockSpec((lb, hd), lambda ni, li: (ni * nl + li, 0)),
                   pl.BlockSpec((lb, LANE), lambda ni, li: (ni * nl + li, 0))],
        out_shape=[jax.ShapeDtypeStruct((t, hd), BF16), jax.ShapeDtypeStruct((t, LANE), F32)],
        compiler_params=_params("parallel", "arbitrary"),
        name=f"dil_attn_g{group}",
    )(*args)


def _dil_out_kernel(*refs, n_heads, dilations):
    ng = len(dilations)
    o_refs, l_refs = refs[:ng], refs[ng:2 * ng]
    x_ref, w_ref, out_ref = refs[2 * ng:2 * ng + 3]
    scratch = list(refs[2 * ng + 3:])
    oc_ref = scratch.pop()
    tm = x_ref.shape[0]

    head_slab, lses = [], []
    for o_ref, l_ref, d in zip(o_refs, l_refs, dilations):
        if d == 1:
            head_slab.append(lambda hd, o_ref=o_ref: o_ref[:, hd * LANE:(hd + 1) * LANE].astype(F32))
            lses.append(l_ref[...])
        else:
            on_ref, ln_ref = scratch.pop(0), scratch.pop(0)
            for r in range(d):
                rows = pl.ds(r, tm // d, stride=d)
                for hd in range(n_heads):
                    on_ref[hd, rows, :] = o_ref[r, :, hd * LANE:(hd + 1) * LANE].astype(F32)
                ln_ref[rows, :] = l_ref[r]
            head_slab.append(lambda hd, on_ref=on_ref: on_ref[hd])
            lses.append(ln_ref[...])

    m = functools.reduce(jnp.maximum, lses)
    es = [jnp.exp(l - m) for l in lses]
    inv = 1.0 / functools.reduce(lambda a, c: a + c, es)
    ws = [e * inv for e in es]
    for hd in range(n_heads):
        oc = None
        for w, slab in zip(ws, head_slab):
            term = w[:, hd:hd + 1] * slab(hd)
            oc = term if oc is None else oc + term
        oc_ref[:, hd * LANE:(hd + 1) * LANE] = oc.astype(BF16)
    out_ref[...] = x_ref[...] + jnp.dot(oc_ref[...], w_ref[...], preferred_element_type=F32)


def _dil_out(os, lses, x2d, w, *, n_heads, dilations, seq, tm):
    t, d_model = x2d.shape
    b = t // seq
    hd = n_heads * LANE
    nbl = seq // tm
    row = lambda i: (i, 0)

    def spec(width, d):
        if d == 1:
            return pl.BlockSpec((tm, width), row)
        return pl.BlockSpec((None, d, tm // d, width), lambda i: (i // nbl, 0, i % nbl, 0))

    view = lambda a, d: a if d == 1 else a.reshape(b, d, seq // d, a.shape[-1])
    scratch = []
    for d in dilations:
        if d != 1:
            scratch += [pltpu.VMEM((n_heads, tm, LANE), F32), pltpu.VMEM((tm, LANE), F32)]
    scratch.append(pltpu.VMEM((tm, hd), BF16))
    return pl.pallas_call(
        functools.partial(_dil_out_kernel, n_heads=n_heads, dilations=dilations),
        grid=(t // tm,),
        in_specs=[spec(hd, d) for d in dilations] + [spec(LANE, d) for d in dilations]
        + [pl.BlockSpec((tm, d_model), row), pl.BlockSpec(w.shape, lambda i: (0, 0))],
        out_specs=pl.BlockSpec((tm, d_model), row),
        out_shape=jax.ShapeDtypeStruct((t, d_model), F32),
        scratch_shapes=scratch,
        compiler_params=_params("parallel"),
        name="dil_out",
    )(*[view(o, d) for o, d in zip(os, dilations)], *[view(l, d) for l, d in zip(lses, dilations)], x2d, w)


def _rope_rows(dim, live):
    half = dim // 2
    inv_freq = jnp.power(ROPE_THETA, -2.0 * jnp.arange(half, dtype=F32) / dim)
    lane = jnp.arange(LANE)
    invf = jnp.where(lane < live, inv_freq[lane % half], 0.0).astype(F32)
    sgn = jnp.where(lane < live, jnp.where(lane % dim < half, -1.0, 1.0), 0.0).astype(F32)
    return invf[None, :], sgn[None, :]


def _pad_lanes(v):
    return jnp.pad(v, (0, LANE - v.shape[0]))[None, :]


def kernel(x, positions, mixer_norm, ffn_norm, mla_w_down, mla_q_norm, mla_kv_norm, mla_w_uq, mla_w_ukv,
           mla_q_gain, mla_k_gain, mla_w_o, dil_w_qkv, dil_q_gain, dil_k_gain, dil_w_o, ffn_w_gate,
           ffn_w_up, ffn_w_down):
    b, s, d = x.shape
    t = b * s
    depth = mixer_norm.shape[0]
    n_heads = d // HEAD_DIM
    x2d = x.reshape(t, d)
    pos = positions.astype(F32).reshape(t, 1)

    for i in range(depth):
        j = i // 2
        gmix = mixer_norm[i][None, :]
        if i % 2 == 0:
            q_lora = mla_q_norm.shape[1]
            kv_lora = mla_kv_norm.shape[1]
            scale = 1.0 / math.sqrt(HEAD_DIM + MLA_ROPE)
            invf, sgn = _rope_rows(MLA_ROPE, MLA_ROPE)
            wd = jnp.pad(mla_w_down[j], ((0, 0), (0, LANE - MLA_ROPE))).astype(BF16)
            wuq = mla_w_uq[j].reshape(q_lora, n_heads, HEAD_DIM + MLA_ROPE)
            wuq = jnp.pad(wuq, ((0, 0), (0, 0), (0, LANE - MLA_ROPE))).reshape(q_lora, n_heads * 2 * LANE)
            wukv = mla_w_ukv[j].reshape(kv_lora, n_heads, 2 * HEAD_DIM)
            wukv = jnp.concatenate([wukv[:, :, :HEAD_DIM].reshape(kv_lora, n_heads * LANE),
                                    wukv[:, :, HEAD_DIM:].reshape(kv_lora, n_heads * LANE)], axis=1)
            qg = mla_q_gain[j] * scale
            kg = mla_k_gain[j]
            qcat, kn, kr, v = _mla_proj(
                x2d, pos, invf, sgn, gmix, wd, mla_q_norm[j][None, :], mla_kv_norm[j][None, :],
                wuq.astype(BF16), wukv.astype(BF16), qg[None, :HEAD_DIM], _pad_lanes(qg[HEAD_DIM:]),
                kg[None, :HEAD_DIM], _pad_lanes(kg[HEAD_DIM:]), n_heads=n_heads, tm=256)
            o = _mla_attn(qcat.reshape(b, s, -1), kn.reshape(b, s, -1), kr.reshape(b, s, -1),
                          v.reshape(b, s, -1), n_heads=n_heads, tq=512, tk=256)
            x2d = _proj_resid(o.reshape(t, -1), mla_w_o[j].astype(BF16), x2d, tm=512)
        else:
            n_groups = len(DIL_PAIRS)
            dilations = tuple(dl for _, dl in DIL_PAIRS)
            scale = 1.0 / math.sqrt(HEAD_DIM)
            half = HEAD_DIM // 2
            invf, _ = _rope_rows(HEAD_DIM, HEAD_DIM)
            sgn = jnp.ones_like(invf)
            gains = jnp.concatenate([dil_q_gain[j] * scale, dil_k_gain[j]], axis=0)
            gains = jnp.stack([jnp.tile(gains[:, :half], (1, 2)), jnp.tile(gains[:, half:], (1, 2))],
                              axis=1).reshape(4 * n_groups, LANE)
            w7 = dil_w_qkv[j].reshape(d, 3, n_groups, n_heads // 2, 2, 2, half)
            w_qkv = jnp.concatenate(
                [w7[:, :2].transpose(0, 1, 2, 3, 5, 4, 6).reshape(d, 2 * n_groups * n_heads * LANE),
                 w7[:, 2].reshape(d, n_groups * n_heads * LANE)], axis=1).astype(BF16)
            seg_ones = jnp.kron(jnp.eye(2 * MXU_DIM // HEAD_DIM, dtype=F32),
                                jnp.ones((half, half), F32)).astype(BF16)
            prepped = _dil_prep(x2d, pos, invf, sgn, gmix, dilations=dilations, seq=s, tm=512)
            os, lses = [], []
            for g, (h_g, cos_g, sin_g) in enumerate(prepped):
                qkv = _dil_qkv(h_g, cos_g, sin_g, w_qkv, gains, seg_ones, group=g, n_groups=n_groups,
                               region=n_heads * LANE, tm=min(t, 1024), tn=min(1024, n_heads * LANE))
                o_g, lse_g = _dil_attn(qkv, group=g, dilation=dilations[g], n_heads=n_heads, seq=s)
                os.append(o_g)
                lses.append(lse_g)
            x2d = _dil_out(os, lses, x2d, dil_w_o[j].astype(BF16), n_heads=n_heads, dilations=dilations,
                           seq=s, tm=256)
        x2d = _ffn(x2d, ffn_norm[i][None, :], ffn_w_gate[i].astype(BF16), ffn_w_up[i].astype(BF16),
                   ffn_w_down[i].astype(BF16), tm=512, tf=512)
    return x2d.reshape(b, s, d)
```

```python
import functools
import math

import jax
import jax.numpy as jnp
from jax import lax
from jax.experimental import pallas as pl
from jax.experimental.pallas import tpu as pltpu

F32 = jnp.float32
BF16 = jnp.bfloat16

EPS = 1e-6
ROPE_THETA = 10000.0
NEG_INF = -1e30

LANE = 128
MXU_DIM = 256
HEAD_DIM = 128
MLA_ROPE = 64
DIL_PAIRS = ((128, 1), (512, 4), (2048, 16))
DIL_SPAN = 128
ATTN_BLK = 128
FFN_TILE = 512
BF16_SUBLANES = 16
LOG2_E = math.log2(math.e)

VMEM_LIMIT_BYTES = 56 * 1024 * 1024


def _params(*sem):
    return pltpu.CompilerParams(dimension_semantics=sem, vmem_limit_bytes=VMEM_LIMIT_BYTES)


def _rms_scale(x, width):
    return lax.rsqrt(jnp.sum(x * x, axis=-1, keepdims=True) * (1.0 / width) + EPS)


def _mla_proj_kernel(x_ref, pos_ref, invf_ref, sgn_ref, gmix_ref, wd_ref, qn_ref, kvn_ref,
                     wuq_ref, wukv_ref, qgn_ref, qgr_ref, kgn_ref, kgr_ref,
                     qcat_ref, kn_ref, kr_ref, v_ref, *, n_heads, q_lora, kv_lora, d_model):
    x = x_ref[...]
    h = (x * _rms_scale(x, d_model) * gmix_ref[...]).astype(BF16)
    down = jnp.dot(h, wd_ref[...], preferred_element_type=F32)
    cq = down[:, :q_lora]
    ckv = down[:, q_lora:q_lora + kv_lora]
    kr_raw = down[:, q_lora + kv_lora:]
    cq = (cq * _rms_scale(cq, q_lora) * qn_ref[...]).astype(BF16)
    ckv = (ckv * _rms_scale(ckv, kv_lora) * kvn_ref[...]).astype(BF16)
    q = jnp.dot(cq, wuq_ref[...], preferred_element_type=F32)
    kv = jnp.dot(ckv, wukv_ref[...], preferred_element_type=F32)

    ang = pos_ref[...] * invf_ref[...]
    cos = jnp.cos(ang)
    sin = jnp.sin(ang) * sgn_ref[...]
    lane = lax.broadcasted_iota(jnp.int32, cos.shape, 1)
    first_half = lane < MLA_ROPE // 2

    def rope(slab):
        partner = jnp.where(first_half, pltpu.roll(slab, LANE - MLA_ROPE // 2, 1),
                            pltpu.roll(slab, MLA_ROPE // 2, 1))
        return slab * cos + partner * sin

    qgn, qgr, kgn, kgr = qgn_ref[...], qgr_ref[...], kgn_ref[...], kgr_ref[...]
    for hd in range(n_heads):
        base = hd * 2 * LANE
        nope = q[:, base:base + LANE]
        qcat_ref[:, base:base + LANE] = (nope * _rms_scale(nope, HEAD_DIM) * qgn).astype(BF16)
        rp = q[:, base + LANE:base + 2 * LANE]
        qcat_ref[:, base + LANE:base + 2 * LANE] = rope(rp * _rms_scale(rp, MLA_ROPE) * qgr).astype(BF16)
        kk = kv[:, hd * LANE:(hd + 1) * LANE]
        kn_ref[:, hd * LANE:(hd + 1) * LANE] = (kk * _rms_scale(kk, HEAD_DIM) * kgn).astype(BF16)
    kr_ref[...] = rope(kr_raw * _rms_scale(kr_raw, MLA_ROPE) * kgr).astype(BF16)
    v_ref[...] = kv[:, n_heads * LANE:].astype(BF16)


def _mla_proj(x2d, pos, invf, sgn, gmix, wd, qn, kvn, wuq, wukv, qgn, qgr, kgn, kgr, *, n_heads, tm):
    t, d = x2d.shape
    q_lora, kv_lora = qn.shape[1], kvn.shape[1]
    hd = n_heads * LANE
    row = lambda i: (i, 0)
    const = lambda i: (0, 0)
    full = lambda a: pl.BlockSpec(a.shape, const)
    return pl.pallas_call(
        functools.partial(_mla_proj_kernel, n_heads=n_heads, q_lora=q_lora, kv_lora=kv_lora, d_model=d),
        grid=(t // tm,),
        in_specs=[pl.BlockSpec((tm, d), row), pl.BlockSpec((tm, 1), row), full(invf), full(sgn), full(gmix),
                  full(wd), full(qn), full(kvn), full(wuq), full(wukv), full(qgn), full(qgr), full(kgn),
                  full(kgr)],
        out_specs=[pl.BlockSpec((tm, 2 * hd), row), pl.BlockSpec((tm, hd), row),
                   pl.BlockSpec((tm, LANE), row), pl.BlockSpec((tm, hd), row)],
        out_shape=[jax.ShapeDtypeStruct((t, 2 * hd), BF16), jax.ShapeDtypeStruct((t, hd), BF16),
                   jax.ShapeDtypeStruct((t, LANE), BF16), jax.ShapeDtypeStruct((t, hd), BF16)],
        compiler_params=_params("parallel"),
        name="mla_proj",
    )(x2d, pos, invf, sgn, gmix, wd, qn, kvn, wuq, wukv, qgn, qgr, kgn, kgr)


def _mla_attn_kernel(q_ref, kn_ref, kr_ref, v_ref, o_ref, s_ref, p_ref, *, tq, tk, heads_per_step):
    seq = q_ref.shape[0]
    contract_last = (((1,), (1,)), ((), ()))
    k_rope = kr_ref[...]
    ones_rows = jnp.ones((BF16_SUBLANES, seq), BF16)
    for hd in range(heads_per_step):
        cs = slice(hd * LANE, (hd + 1) * LANE)
        v_t = jnp.concatenate([v_ref[:, cs].astype(F32).T.astype(BF16), ones_rows], axis=0)
        for qi in range(seq // tq):
            qrows = slice(qi * tq, (qi + 1) * tq)
            kv = (qi + 1) * tq
            q = q_ref[qrows, 2 * hd * LANE:2 * (hd + 1) * LANE]
            k = jnp.concatenate([kn_ref[0:kv, cs], k_rope[0:kv]], axis=-1)
            s_ref[0:kv, :] = lax.dot_general(k, q, contract_last, preferred_element_type=F32)

            def chunk(j):
                s = s_ref[j * tk:(j + 1) * tk, :]
                if (j + 1) * tk > qi * tq:
                    key = j * tk + lax.broadcasted_iota(jnp.int32, (tk, tq), 0)
                    qry = qi * tq + lax.broadcasted_iota(jnp.int32, (tk, tq), 1)
                    s = jnp.where(key <= qry, s, NEG_INF)
                return s

            m = None
            for j in range(kv // tk):
                mj = jnp.max(chunk(j), axis=0, keepdims=True)
                m = mj if m is None else jnp.maximum(m, mj)
            for j in range(kv // tk):
                p_ref[j * tk:(j + 1) * tk, :] = jnp.exp2(chunk(j) - m).astype(BF16)
            o_t = jnp.dot(v_t[:, :kv], p_ref[0:kv, :], preferred_element_type=F32)
            o_ref[qrows, cs] = (o_t[:HEAD_DIM] / o_t[HEAD_DIM:HEAD_DIM + 1]).T.astype(BF16)


def _mla_attn(qcat, kn, kr, v, *, n_heads, tq, tk, heads_per_step):
    b, s, _ = kn.shape
    hw = heads_per_step * LANE
    head = lambda bi, hi: (bi, 0, hi)
    return pl.pallas_call(
        functools.partial(_mla_attn_kernel, tq=tq, tk=tk, heads_per_step=heads_per_step),
        grid=(b, n_heads // heads_per_step),
        in_specs=[pl.BlockSpec((None, s, 2 * hw), head), pl.BlockSpec((None, s, hw), head),
                  pl.BlockSpec((None, s, LANE), lambda bi, hi: (bi, 0, 0)),
                  pl.BlockSpec((None, s, hw), head)],
        out_specs=pl.BlockSpec((None, s, hw), head),
        out_shape=jax.ShapeDtypeStruct((b, s, n_heads * LANE), BF16),
        scratch_shapes=[pltpu.VMEM((s, tq), F32), pltpu.VMEM((s, tq), BF16)],
        compiler_params=_params("parallel", "parallel"),
        name="mla_attn",
    )(qcat, kn, kr, v)


def _proj_resid_kernel(a_ref, w_ref, r_ref, o_ref):
    o_ref[...] = r_ref[...] + jnp.dot(a_ref[...], w_ref[...], preferred_element_type=F32)


def _proj_resid(a, w, resid, *, tm):
    t, k = a.shape
    n = w.shape[1]
    return pl.pallas_call(
        _proj_resid_kernel,
        grid=(t // tm,),
        in_specs=[pl.BlockSpec((tm, k), lambda i: (i, 0)), pl.BlockSpec((k, n), lambda i: (0, 0)),
                  pl.BlockSpec((tm, n), lambda i: (i, 0))],
        out_specs=pl.BlockSpec((tm, n), lambda i: (i, 0)),
        out_shape=jax.ShapeDtypeStruct((t, n), F32),
        compiler_params=_params("parallel"),
        name="proj_resid",
    )(a, w, resid)


def _ffn_kernel(x_ref, g_ref, wg_ref, wu_ref, wd_ref, o_ref, h_ref, *, d_model):
    @pl.when(pl.program_id(1) == 0)
    def _():
        x = x_ref[...]
        h_ref[...] = (x * _rms_scale(x, d_model) * g_ref[...]).astype(BF16)
        o_ref[...] = x

    h = h_ref[...]
    gate = jnp.dot(h, wg_ref[...], preferred_element_type=F32)
    up = jnp.dot(h, wu_ref[...], preferred_element_type=F32)
    a = (gate * jax.nn.sigmoid(gate) * up).astype(BF16)
    o_ref[...] += jnp.dot(a, wd_ref[...], preferred_element_type=F32)


def _column_tiles(w, tn):
    k, n = w.shape
    return w.reshape(k, n // tn, tn).transpose(1, 0, 2)


def _ffn(x2d, gain, wg, wu, wd, *, tm):
    t, d = x2d.shape
    nf, _, tf = wg.shape
    return pl.pallas_call(
        functools.partial(_ffn_kernel, d_model=d),
        grid=(t // tm, nf),
        in_specs=[pl.BlockSpec((tm, d), lambda i, j: (i, 0)), pl.BlockSpec((1, d), lambda i, j: (0, 0)),
                  pl.BlockSpec((None, d, tf), lambda i, j: (j, 0, 0)),
                  pl.BlockSpec((None, d, tf), lambda i, j: (j, 0, 0)),
                  pl.BlockSpec((tf, d), lambda i, j: (j, 0))],
        out_specs=pl.BlockSpec((tm, d), lambda i, j: (i, 0)),
        out_shape=jax.ShapeDtypeStruct((t, d), F32),
        scratch_shapes=[pltpu.VMEM((tm, d), BF16)],
        compiler_params=_params("parallel", "arbitrary"),
        name="ffn",
    )(x2d, gain, wg, wu, wd)


def _dil_prep_kernel(x_ref, pos_ref, invf_ref, sgn_ref, g_ref, *refs, dilations, d_model):
    n_out = 3 * len(dilations)
    outs, (h_s, cos_s, sin_s) = refs[:n_out], refs[n_out:]
    tm = x_ref.shape[0]
    n_tiles = d_model // LANE
    x = x_ref[...]
    h = x * _rms_scale(x, d_model) * g_ref[...]
    for c in range(n_tiles):
        h_s[c] = h[:, c * LANE:(c + 1) * LANE]
    ang = pos_ref[...] * invf_ref[...]
    cos_s[...] = jnp.cos(ang)
    sin_s[...] = jnp.sin(ang) * sgn_ref[...]
    for gi, d in enumerate(dilations):
        h_o, cos_o, sin_o = outs[3 * gi:3 * gi + 3]
        if d == 1:
            h_o[...] = h.astype(BF16)
            cos_o[...] = cos_s[...]
            sin_o[...] = sin_s[...]
        else:
            for r in range(d):
                rows = pl.ds(r, tm // d, stride=d)
                for c in range(n_tiles):
                    h_o[r, :, c * LANE:(c + 1) * LANE] = h_s[c, rows, :].astype(BF16)
                cos_o[r] = cos_s[rows, :]
                sin_o[r] = sin_s[rows, :]


def _dil_prep(x2d, pos, invf, sgn, gain, *, dilations, seq, tm):
    t, d_model = x2d.shape
    b = t // seq
    nbl = seq // tm
    row = lambda i: (i, 0)
    const = lambda i: (0, 0)
    out_specs, out_shape = [], []
    for d in dilations:
        for width, dtype in ((d_model, BF16), (LANE, F32), (LANE, F32)):
            if d == 1:
                out_specs.append(pl.BlockSpec((tm, width), row))
                out_shape.append(jax.ShapeDtypeStruct((t, width), dtype))
            else:
                out_specs.append(pl.BlockSpec((None, d, tm // d, width), lambda i: (i // nbl, 0, i % nbl, 0)))
                out_shape.append(jax.ShapeDtypeStruct((b, d, seq // d, width), dtype))
    outs = pl.pallas_call(
        functools.partial(_dil_prep_kernel, dilations=dilations, d_model=d_model),
        grid=(t // tm,),
        in_specs=[pl.BlockSpec((tm, d_model), row), pl.BlockSpec((tm, 1), row), pl.BlockSpec(invf.shape, const),
                  pl.BlockSpec(sgn.shape, const), pl.BlockSpec(gain.shape, const)],
        out_specs=out_specs,
        out_shape=out_shape,
        scratch_shapes=[pltpu.VMEM((d_model // LANE, tm, LANE), F32), pltpu.VMEM((tm, LANE), F32),
                        pltpu.VMEM((tm, LANE), F32)],
        compiler_params=_params("parallel"),
        name="dil_prep",
    )(x2d, pos, invf, sgn, gain)
    outs = [o.reshape(t, o.shape[-1]) for o in outs]
    return [tuple(outs[3 * gi:3 * gi + 3]) for gi in range(len(dilations))]


def _dil_qkv_kernel(h_ref, cos_ref, sin_ref, w_ref, gains_ref, seg_ref, o_ref, *, tn, tiles_per_region,
                    group, n_groups):
    j = pl.program_id(1)

    @pl.when(j < 2 * tiles_per_region)
    def _():
        region = (j // tiles_per_region) * n_groups + group
        g_lo = gains_ref[pl.ds(2 * region, 1), :]
        g_hi = gains_ref[pl.ds(2 * region + 1, 1), :]
        cos, sin = cos_ref[...], sin_ref[...]
        a_cos, b_sin, b_cos, a_sin = g_lo * cos, g_hi * sin, g_hi * cos, g_lo * sin
        y = jnp.dot(h_ref[...], w_ref[...], preferred_element_type=F32)
        for c in range(0, tn, 2 * MXU_DIM):
            slabs = [y[:, c + k * LANE:c + (k + 1) * LANE] for k in range(4)]
            sq = jnp.concatenate([slabs[0] * slabs[0] + slabs[1] * slabs[1],
                                  slabs[2] * slabs[2] + slabs[3] * slabs[3]], axis=-1)
            ss = jnp.dot(sq.astype(BF16), seg_ref[...], preferred_element_type=F32)
            r = lax.rsqrt(ss * (1.0 / HEAD_DIM) + EPS)
            for p in range(2):
                a, b = slabs[2 * p], slabs[2 * p + 1]
                rp = r[:, p * LANE:(p + 1) * LANE]
                base = c + p * MXU_DIM
                o_ref[:, base:base + LANE] = ((a * a_cos - b * b_sin) * rp).astype(BF16)
                o_ref[:, base + LANE:base + 2 * LANE] = ((b * b_cos + a * a_sin) * rp).astype(BF16)

    @pl.when(j >= 2 * tiles_per_region)
    def _():
        o_ref[...] = jnp.dot(h_ref[...], w_ref[...], preferred_element_type=F32).astype(BF16)


def _dil_qkv(h, cos, sin, w, gains, seg_ones, *, group, n_groups, region, tm):
    t, d = h.shape
    tn = w.shape[2]
    tpr = region // tn
    wtile = lambda i, j: (((j // tpr) * n_groups + group) * tpr + j % tpr, 0, 0)
    return pl.pallas_call(
        functools.partial(_dil_qkv_kernel, tn=tn, tiles_per_region=tpr, group=group, n_groups=n_groups),
        grid=(t // tm, 3 * tpr),
        in_specs=[pl.BlockSpec((tm, d), lambda i, j: (i, 0)), pl.BlockSpec((tm, LANE), lambda i, j: (i, 0)),
                  pl.BlockSpec((tm, LANE), lambda i, j: (i, 0)), pl.BlockSpec((None, d, tn), wtile),
                  pl.BlockSpec(gains.shape, lambda i, j: (0, 0)),
                  pl.BlockSpec(seg_ones.shape, lambda i, j: (0, 0))],
        out_specs=pl.BlockSpec((tm, tn), lambda i, j: (i, j)),
        out_shape=jax.ShapeDtypeStruct((t, 3 * region), BF16),
        compiler_params=_params("parallel", "arbitrary"),
        name=f"dil_qkv_g{group}",
    )(h, cos, sin, w, gains, seg_ones)


def _dil_attn_kernel(*refs, lb, n_heads, has_prev):
    if has_prev:
        q_ref, k_ref, v_ref, kp_ref, vp_ref, o_ref, lse_ref = refs
    else:
        q_ref, k_ref, v_ref, o_ref, lse_ref = refs
    blk = ATTN_BLK
    not_first = pl.program_id(1) > 0
    ii = lax.broadcasted_iota(jnp.int32, (blk, 2 * blk), 0)
    jj = lax.broadcasted_iota(jnp.int32, (blk, 2 * blk), 1)
    band = (jj >= ii) & (jj <= ii + DIL_SPAN)
    bias_band = jnp.where(band, 0.0, NEG_INF).astype(F32)
    bias_head = jnp.where(band & (jj >= blk), 0.0, NEG_INF).astype(F32)
    bias_tri = bias_head[:, blk:]
    lane = lax.broadcasted_iota(jnp.int32, (blk, LANE), 1)
    contract_last = (((1,), (1,)), ((), ()))

    pair_lane = lax.broadcasted_iota(jnp.int32, (blk, 2 * LANE), 1) % LANE
    lo_mask = jnp.where(pair_lane < HEAD_DIM // 2, 1.0, 0.0).astype(BF16)
    pair_mask = jnp.concatenate([lo_mask, 1.0 - lo_mask], axis=0)
    stack2 = lambda a: jnp.concatenate([a, a], axis=0)
    bias_band2, bias_head2, bias_tri2 = stack2(bias_band), stack2(bias_head), stack2(bias_tri)

    for i in range(lb // blk):
        rows = slice(i * blk, (i + 1) * blk)
        krows = slice((i - 1) * blk, (i + 1) * blk)
        lse_tile = jnp.zeros((blk, LANE), F32)
        for pr in range(n_heads // 2):
            ps = slice(pr * 2 * LANE, (pr + 1) * 2 * LANE)
            q2 = stack2(q_ref[rows, ps]) * pair_mask
            if i > 0:
                k = k_ref[krows, ps]
                bias = bias_band2
            elif has_prev:
                k = jnp.concatenate([kp_ref[:, ps], k_ref[rows, ps]], axis=0)
                bias = jnp.where(not_first, bias_band2, bias_head2)
            else:
                k = k_ref[rows, ps]
                bias = bias_tri2
            s = lax.dot_general(q2, k, contract_last, preferred_element_type=F32) + bias
            m = jnp.max(s, axis=-1, keepdims=True)
            p = jnp.exp2(s - m)
            l = jnp.sum(p, axis=-1, keepdims=True)
            lse = m + jnp.log2(l)
            p = p.astype(BF16)
            for sub in range(2):
                hd = 2 * pr + sub
                cs = slice(hd * LANE, (hd + 1) * LANE)
                hr = slice(sub * blk, (sub + 1) * blk)
                if i > 0:
                    v = v_ref[krows, cs]
                elif has_prev:
                    v = jnp.concatenate([vp_ref[:, cs], v_ref[rows, cs]], axis=0)
                else:
                    v = v_ref[rows, cs]
                o = jnp.dot(p[hr], v, preferred_element_type=F32) / l[hr]
                o_ref[rows, cs] = o.astype(BF16)
                lse_tile = jnp.where(lane == hd, lse[hr], lse_tile)
        lse_ref[rows, :] = lse_tile


def _dil_attn(qkv, *, group, dilation, n_heads, seq):
    t, n = qkv.shape
    hd = n_heads * LANE
    l = seq // dilation
    lb = min(l, 2 * ATTN_BLK)
    nsub = lb // ATTN_BLK
    nl = l // lb
    has_prev = l > lb
    cur = lambda c: pl.BlockSpec((lb, hd), lambda ni, li: (ni * nl + li, c))
    prev = lambda c: pl.BlockSpec(
        (ATTN_BLK, hd), lambda ni, li: (ni * nl * nsub + jnp.maximum(li * nsub - 1, 0), c))
    in_specs = [cur(0), cur(1), cur(2)]
    args = [qkv, qkv, qkv]
    if has_prev:
        in_specs += [prev(1), prev(2)]
        args += [qkv, qkv]
    return pl.pallas_call(
        functools.partial(_dil_attn_kernel, lb=lb, n_heads=n_heads, has_prev=has_prev),
        grid=(t // l, nl),
        in_specs=in_specs,
        out_specs=[pl.BlockSpec((lb, hd), lambda ni, li: (ni * nl + li, 0)),
                   pl.BlockSpec((lb, LANE), lambda ni, li: (ni * nl + li, 0))],
        out_shape=[jax.ShapeDtypeStruct((t, hd), BF16), jax.ShapeDtypeStruct((t, LANE), F32)],
        compiler_params=_params("parallel", "arbitrary"),
        name=f"dil_attn_g{group}",
    )(*args)


def _dil_out_kernel(*refs, n_heads, dilations, n_split):
    ng = len(dilations)
    o_refs, l_refs = refs[:ng], refs[ng:2 * ng]
    x_ref, w_ref, out_ref = refs[2 * ng:2 * ng + 3]
    scratch = list(refs[2 * ng + 3:])
    oc_ref = scratch.pop()
    tm = x_ref.shape[0]
    th = tm // n_split

    for base in range(0, tm, th):
        head_slab, lses = [], []
        for gi, (o_ref, l_ref, d) in enumerate(zip(o_refs, l_refs, dilations)):
            if d == 1:
                head_slab.append(lambda hd, o_ref=o_ref, base=base:
                                 o_ref[base:base + th, hd * LANE:(hd + 1) * LANE].astype(F32))
                lses.append(l_ref[base:base + th, :])
            else:
                si = 2 * sum(1 for dd in dilations[:gi] if dd != 1)
                on_ref, ln_ref = scratch[si], scratch[si + 1]
                src = slice(base // d, (base + th) // d)
                for r in range(d):
                    rows = pl.ds(base + r, th // d, stride=d)
                    for hd in range(n_heads):
                        on_ref[hd, rows, :] = o_ref[r, src, hd * LANE:(hd + 1) * LANE].astype(F32)
                    ln_ref[rows, :] = l_ref[r, src, :]
                head_slab.append(lambda hd, on_ref=on_ref, base=base: on_ref[hd, base:base + th, :])
                lses.append(ln_ref[base:base + th, :])

        m = functools.reduce(jnp.maximum, lses)
        es = [jnp.exp2(l - m) for l in lses]
        inv = 1.0 / functools.reduce(lambda a, c: a + c, es)
        ws = [e * inv for e in es]
        for hd in range(n_heads):
            oc = None
            for w, slab in zip(ws, head_slab):
                term = w[:, hd:hd + 1] * slab(hd)
                oc = term if oc is None else oc + term
            oc_ref[base:base + th, hd * LANE:(hd + 1) * LANE] = oc.astype(BF16)
        out_ref[base:base + th, :] = x_ref[base:base + th, :] + jnp.dot(
            oc_ref[base:base + th, :], w_ref[...], preferred_element_type=F32)


def _dil_out(os, lses, x2d, w, *, n_heads, dilations, seq, tm, n_split):
    t, d_model = x2d.shape
    b = t // seq
    hd = n_heads * LANE
    nbl = seq // tm
    row = lambda i: (i, 0)

    def spec(width, d):
        if d == 1:
            return pl.BlockSpec((tm, width), row)
        return pl.BlockSpec((None, d, tm // d, width), lambda i: (i // nbl, 0, i % nbl, 0))

    view = lambda a, d: a if d == 1 else a.reshape(b, d, seq // d, a.shape[-1])
    scratch = []
    for d in dilations:
        if d != 1:
            scratch += [pltpu.VMEM((n_heads, tm, LANE), F32), pltpu.VMEM((tm, LANE), F32)]
    scratch.append(pltpu.VMEM((tm, hd), BF16))
    return pl.pallas_call(
        functools.partial(_dil_out_kernel, n_heads=n_heads, dilations=dilations, n_split=n_split),
        grid=(t // tm,),
        in_specs=[spec(hd, d) for d in dilations] + [spec(LANE, d) for d in dilations]
        + [pl.BlockSpec((tm, d_model), row),
           pl.BlockSpec(w.shape, lambda i: (0, 0), pipeline_mode=pl.Buffered(1))],
        out_specs=pl.BlockSpec((tm, d_model), row),
        out_shape=jax.ShapeDtypeStruct((t, d_model), F32),
        scratch_shapes=scratch,
        compiler_params=_params("parallel"),
        name="dil_out",
    )(*[view(o, d) for o, d in zip(os, dilations)], *[view(l, d) for l, d in zip(lses, dilations)], x2d, w)


def _rope_rows(dim, live):
    half = dim // 2
    inv_freq = jnp.power(ROPE_THETA, -2.0 * jnp.arange(half, dtype=F32) / dim)
    lane = jnp.arange(LANE)
    invf = jnp.where(lane < live, inv_freq[lane % half], 0.0).astype(F32)
    sgn = jnp.where(lane < live, jnp.where(lane % dim < half, -1.0, 1.0), 0.0).astype(F32)
    return invf[None, :], sgn[None, :]


def _pad_lanes(v):
    return jnp.pad(v, (0, LANE - v.shape[0]))[None, :]


def kernel(x, positions, mixer_norm, ffn_norm, mla_w_down, mla_q_norm, mla_kv_norm, mla_w_uq, mla_w_ukv,
           mla_q_gain, mla_k_gain, mla_w_o, dil_w_qkv, dil_q_gain, dil_k_gain, dil_w_o, ffn_w_gate,
           ffn_w_up, ffn_w_down):
    b, s, d = x.shape
    t = b * s
    depth = mixer_norm.shape[0]
    n_heads = d // HEAD_DIM
    x2d = x.reshape(t, d)
    pos = positions.astype(F32).reshape(t, 1)

    for i in range(depth):
        j = i // 2
        gmix = mixer_norm[i][None, :]
        if i % 2 == 0:
            q_lora = mla_q_norm.shape[1]
            kv_lora = mla_kv_norm.shape[1]
            scale = LOG2_E / math.sqrt(HEAD_DIM + MLA_ROPE)
            invf, sgn = _rope_rows(MLA_ROPE, MLA_ROPE)
            wd = jnp.pad(mla_w_down[j], ((0, 0), (0, LANE - MLA_ROPE))).astype(BF16)
            wuq = mla_w_uq[j].reshape(q_lora, n_heads, HEAD_DIM + MLA_ROPE)
            wuq = jnp.pad(wuq, ((0, 0), (0, 0), (0, LANE - MLA_ROPE))).reshape(q_lora, n_heads * 2 * LANE)
            wukv = mla_w_ukv[j].reshape(kv_lora, n_heads, 2 * HEAD_DIM)
            wukv = jnp.concatenate([wukv[:, :, :HEAD_DIM].reshape(kv_lora, n_heads * LANE),
                                    wukv[:, :, HEAD_DIM:].reshape(kv_lora, n_heads * LANE)], axis=1)
            qg = mla_q_gain[j] * scale
            kg = mla_k_gain[j]
            qcat, kn, kr, v = _mla_proj(
                x2d, pos, invf, sgn, gmix, wd, mla_q_norm[j][None, :], mla_kv_norm[j][None, :],
                wuq.astype(BF16), wukv.astype(BF16), qg[None, :HEAD_DIM], _pad_lanes(qg[HEAD_DIM:]),
                kg[None, :HEAD_DIM], _pad_lanes(kg[HEAD_DIM:]), n_heads=n_heads, tm=256)
            o = _mla_attn(qcat.reshape(b, s, -1), kn.reshape(b, s, -1), kr.reshape(b, s, -1),
                          v.reshape(b, s, -1), n_heads=n_heads, tq=512, tk=256, heads_per_step=2)
            x2d = _proj_resid(o.reshape(t, -1), mla_w_o[j].astype(BF16), x2d, tm=512)
        else:
            n_groups = len(DIL_PAIRS)
            dilations = tuple(dl for _, dl in DIL_PAIRS)
            scale = LOG2_E / math.sqrt(HEAD_DIM)
            half = HEAD_DIM // 2
            invf, _ = _rope_rows(HEAD_DIM, HEAD_DIM)
            sgn = jnp.ones_like(invf)
            gains = jnp.concatenate([dil_q_gain[j] * scale, dil_k_gain[j]], axis=0)
            gains = jnp.stack([jnp.tile(gains[:, :half], (1, 2)), jnp.tile(gains[:, half:], (1, 2))],
                              axis=1).reshape(4 * n_groups, LANE)
            w7 = dil_w_qkv[j].reshape(d, 3, n_groups, n_heads // 2, 2, 2, half)
            w_qkv = jnp.concatenate(
                [w7[:, :2].transpose(0, 1, 2, 3, 5, 4, 6).reshape(d, 2 * n_groups * n_heads * LANE),
                 w7[:, 2].reshape(d, n_groups * n_heads * LANE)], axis=1).astype(BF16)
            w_qkv = _column_tiles(w_qkv, min(1024, n_heads * LANE))
            seg_ones = jnp.kron(jnp.eye(2 * MXU_DIM // HEAD_DIM, dtype=F32),
                                jnp.ones((half, half), F32)).astype(BF16)
            prepped = _dil_prep(x2d, pos, invf, sgn, gmix, dilations=dilations, seq=s, tm=512)
            os, lses = [], []
            for g, (h_g, cos_g, sin_g) in enumerate(prepped):
                qkv = _dil_qkv(h_g, cos_g, sin_g, w_qkv, gains, seg_ones, group=g, n_groups=n_groups,
                               region=n_heads * LANE, tm=min(t, 1024))
                o_g, lse_g = _dil_attn(qkv, group=g, dilation=dilations[g], n_heads=n_heads, seq=s)
                os.append(o_g)
                lses.append(lse_g)
            x2d = _dil_out(os, lses, x2d, dil_w_o[j].astype(BF16), n_heads=n_heads, dilations=dilations,
                           seq=s, tm=512, n_split=2)
        x2d = _ffn(x2d, ffn_norm[i][None, :], _column_tiles(ffn_w_gate[i].astype(BF16), FFN_TILE),
                   _column_tiles(ffn_w_up[i].astype(BF16), FFN_TILE), ffn_w_down[i].astype(BF16),
                   tm=min(t, 1024))
    return x2d.reshape(b, s, d)
```

```python
import functools
import math

import jax
import jax.numpy as jnp
from jax import lax
from jax.experimental import pallas as pl
from jax.experimental.pallas import tpu as pltpu

F32 = jnp.float32
BF16 = jnp.bfloat16

EPS = 1e-6
ROPE_THETA = 10000.0
NEG_INF = -1e30

LANE = 128
MXU_DIM = 256
HEAD_DIM = 128
MLA_ROPE = 64
DIL_PAIRS = ((128, 1), (512, 4), (2048, 16))
DIL_SPAN = 128
ATTN_BLK = 128
FFN_TILE = 512
BF16_SUBLANES = 16
LOG2_E = math.log2(math.e)

VMEM_LIMIT_BYTES = 56 * 1024 * 1024


def _params(*sem):
    return pltpu.CompilerParams(dimension_semantics=sem, vmem_limit_bytes=VMEM_LIMIT_BYTES)


def _rms_scale(x, width):
    return lax.rsqrt(jnp.sum(x * x, axis=-1, keepdims=True) * (1.0 / width) + EPS)


def _mla_proj_kernel(x_ref, pos_ref, invf_ref, sgn_ref, gmix_ref, wd_ref, qn_ref, kvn_ref,
                     wuq_ref, wukv_ref, qgn_ref, qgr_ref, kgn_ref, kgr_ref,
                     qcat_ref, kn_ref, kr_ref, v_ref, *, n_heads, q_lora, kv_lora, d_model):
    x = x_ref[...]
    h = (x * _rms_scale(x, d_model) * gmix_ref[...]).astype(BF16)
    down = jnp.dot(h, wd_ref[...], preferred_element_type=F32)
    cq = down[:, :q_lora]
    ckv = down[:, q_lora:q_lora + kv_lora]
    kr_raw = down[:, q_lora + kv_lora:]
    cq = (cq * _rms_scale(cq, q_lora) * qn_ref[...]).astype(BF16)
    ckv = (ckv * _rms_scale(ckv, kv_lora) * kvn_ref[...]).astype(BF16)
    q = jnp.dot(cq, wuq_ref[...], preferred_element_type=F32)
    kv = jnp.dot(ckv, wukv_ref[...], preferred_element_type=F32)

    ang = pos_ref[...] * invf_ref[...]
    cos = jnp.cos(ang)
    sin = jnp.sin(ang) * sgn_ref[...]

    def rope(slab):
        return slab * cos + pltpu.roll(slab, LANE // 2, 1) * sin

    qgn, qgr, kgn, kgr = qgn_ref[...], qgr_ref[...], kgn_ref[...], kgr_ref[...]
    for hd in range(n_heads):
        base = hd * 2 * LANE
        nope = q[:, base:base + LANE]
        qcat_ref[:, base:base + LANE] = (nope * _rms_scale(nope, HEAD_DIM) * qgn).astype(BF16)
        rp = q[:, base + LANE:base + 2 * LANE]
        qcat_ref[:, base + LANE:base + 2 * LANE] = rope(rp * _rms_scale(rp, MLA_ROPE) * qgr).astype(BF16)
        kk = kv[:, hd * LANE:(hd + 1) * LANE]
        kn_ref[:, hd * LANE:(hd + 1) * LANE] = (kk * _rms_scale(kk, HEAD_DIM) * kgn).astype(BF16)
    kr_ref[...] = rope(kr_raw * _rms_scale(kr_raw, MLA_ROPE) * kgr).astype(BF16)
    v_ref[...] = kv[:, n_heads * LANE:].astype(BF16)


def _mla_proj(x2d, pos, invf, sgn, gmix, wd, qn, kvn, wuq, wukv, qgn, qgr, kgn, kgr, *, n_heads, tm):
    t, d = x2d.shape
    q_lora, kv_lora = qn.shape[1], kvn.shape[1]
    hd = n_heads * LANE
    row = lambda i: (i, 0)
    const = lambda i: (0, 0)
    full = lambda a: pl.BlockSpec(a.shape, const)
    return pl.pallas_call(
        functools.partial(_mla_proj_kernel, n_heads=n_heads, q_lora=q_lora, kv_lora=kv_lora, d_model=d),
        grid=(t // tm,),
        in_specs=[pl.BlockSpec((tm, d), row), pl.BlockSpec((tm, 1), row), full(invf), full(sgn), full(gmix),
                  full(wd), full(qn), full(kvn), full(wuq), full(wukv), full(qgn), full(qgr), full(kgn),
                  full(kgr)],
        out_specs=[pl.BlockSpec((tm, 2 * hd), row), pl.BlockSpec((tm, hd), row),
                   pl.BlockSpec((tm, LANE), row), pl.BlockSpec((tm, hd), row)],
        out_shape=[jax.ShapeDtypeStruct((t, 2 * hd), BF16), jax.ShapeDtypeStruct((t, hd), BF16),
                   jax.ShapeDtypeStruct((t, LANE), BF16), jax.ShapeDtypeStruct((t, hd), BF16)],
        compiler_params=_params("parallel"),
        name="mla_proj",
    )(x2d, pos, invf, sgn, gmix, wd, qn, kvn, wuq, wukv, qgn, qgr, kgn, kgr)


def _mla_attn_kernel(q_ref, kn_ref, kr_ref, v_ref, o_ref, s_ref, p_ref, *, tq, tk, heads_per_step):
    seq = q_ref.shape[0]
    contract_last = (((1,), (1,)), ((), ()))
    k_rope = kr_ref[...]
    ones_rows = jnp.ones((BF16_SUBLANES, seq), BF16)
    for hd in range(heads_per_step):
        cs = slice(hd * LANE, (hd + 1) * LANE)
        v_t = jnp.concatenate([v_ref[:, cs].astype(F32).T.astype(BF16), ones_rows], axis=0)
        for qi in range(seq // tq):
            qrows = slice(qi * tq, (qi + 1) * tq)
            kv = (qi + 1) * tq
            q = q_ref[qrows, 2 * hd * LANE:2 * (hd + 1) * LANE]
            k = jnp.concatenate([kn_ref[0:kv, cs], k_rope[0:kv]], axis=-1)
            s_ref[0:kv, :] = lax.dot_general(k, q, contract_last, preferred_element_type=F32)

            def chunk(j):
                s = s_ref[j * tk:(j + 1) * tk, :]
                if (j + 1) * tk > qi * tq:
                    key = j * tk + lax.broadcasted_iota(jnp.int32, (tk, tq), 0)
                    qry = qi * tq + lax.broadcasted_iota(jnp.int32, (tk, tq), 1)
                    s = jnp.where(key <= qry, s, NEG_INF)
                return s

            m = None
            for j in range(kv // tk):
                mj = jnp.max(chunk(j), axis=0, keepdims=True)
                m = mj if m is None else jnp.maximum(m, mj)
            for j in range(kv // tk):
                p_ref[j * tk:(j + 1) * tk, :] = jnp.exp2(chunk(j) - m).astype(BF16)
            o_t = jnp.dot(v_t[:, :kv], p_ref[0:kv, :], preferred_element_type=F32)
            o_ref[qrows, cs] = (o_t[:HEAD_DIM] / o_t[HEAD_DIM:HEAD_DIM + 1]).T.astype(BF16)


def _mla_attn(qcat, kn, kr, v, *, n_heads, tq, tk, heads_per_step):
    b, s, _ = kn.shape
    hw = heads_per_step * LANE
    head = lambda bi, hi: (bi, 0, hi)
    return pl.pallas_call(
        functools.partial(_mla_attn_kernel, tq=tq, tk=tk, heads_per_step=heads_per_step),
        grid=(b, n_heads // heads_per_step),
        in_specs=[pl.BlockSpec((None, s, 2 * hw), head), pl.BlockSpec((None, s, hw), head),
                  pl.BlockSpec((None, s, LANE), lambda bi, hi: (bi, 0, 0)),
                  pl.BlockSpec((None, s, hw), head)],
        out_specs=pl.BlockSpec((None, s, hw), head),
        out_shape=jax.ShapeDtypeStruct((b, s, n_heads * LANE), BF16),
        scratch_shapes=[pltpu.VMEM((s, tq), F32), pltpu.VMEM((s, tq), BF16)],
        compiler_params=_params("parallel", "parallel"),
        name="mla_attn",
    )(qcat, kn, kr, v)


def _proj_resid_kernel(a_ref, w_ref, r_ref, o_ref):
    o_ref[...] = r_ref[...] + jnp.dot(a_ref[...], w_ref[...], preferred_element_type=F32)


def _proj_resid(a, w, resid, *, tm):
    t, k = a.shape
    n = w.shape[1]
    return pl.pallas_call(
        _proj_resid_kernel,
        grid=(t // tm,),
        in_specs=[pl.BlockSpec((tm, k), lambda i: (i, 0)), pl.BlockSpec((k, n), lambda i: (0, 0)),
                  pl.BlockSpec((tm, n), lambda i: (i, 0))],
        out_specs=pl.BlockSpec((tm, n), lambda i: (i, 0)),
        out_shape=jax.ShapeDtypeStruct((t, n), F32),
        compiler_params=_params("parallel"),
        name="proj_resid",
    )(a, w, resid)


def _ffn_kernel(x_ref, g_ref, wg_ref, wu_ref, wd_ref, o_ref, h_ref, *, d_model):
    @pl.when(pl.program_id(1) == 0)
    def _():
        x = x_ref[...]
        h_ref[...] = (x * _rms_scale(x, d_model) * g_ref[...]).astype(BF16)
        o_ref[...] = x

    h = h_ref[...]
    gate = jnp.dot(h, wg_ref[...], preferred_element_type=F32)
    up = jnp.dot(h, wu_ref[...], preferred_element_type=F32)
    a = (gate * jax.nn.sigmoid(gate) * up).astype(BF16)
    o_ref[...] += jnp.dot(a, wd_ref[...], preferred_element_type=F32)


def _ffn_weights_kernel(wg_ref, wu_ref, wd_ref, og_ref, ou_ref, od_ref):
    og_ref[...] = wg_ref[...].astype(BF16)
    ou_ref[...] = wu_ref[...].astype(BF16)
    od_ref[...] = wd_ref[...].astype(BF16)


def _ffn_weights(w_gate, w_up, w_down, *, tf):
    n_layers, d, f = w_gate.shape
    nf = f // tf
    col_in = pl.BlockSpec((None, d, tf), lambda l, j: (l, 0, j))
    col_out = pl.BlockSpec((None, None, d, tf), lambda l, j: (l, j, 0, 0))
    row = pl.BlockSpec((None, tf, d), lambda l, j: (l, j, 0))
    return pl.pallas_call(
        _ffn_weights_kernel,
        grid=(n_layers, nf),
        in_specs=[col_in, col_in, row],
        out_specs=[col_out, col_out, row],
        out_shape=[jax.ShapeDtypeStruct((n_layers, nf, d, tf), BF16)] * 2
        + [jax.ShapeDtypeStruct((n_layers, f, d), BF16)],
        compiler_params=_params("parallel", "parallel"),
        name="ffn_weights",
    )(w_gate, w_up, w_down)


def _qkv_weights_kernel(w_ref, o_ref, *, n_pair_tiles):
    tn = w_ref.shape[1]

    @pl.when(pl.program_id(0) < n_pair_tiles)
    def _():
        lane = lax.broadcasted_iota(jnp.int32, (w_ref.shape[0], LANE), 1)
        low = lane < HEAD_DIM // 2
        for c in range(0, tn, 2 * LANE):
            head_a = w_ref[:, c:c + LANE]
            head_b = w_ref[:, c + LANE:c + 2 * LANE]
            o_ref[:, c:c + LANE] = jnp.where(low, head_a, pltpu.roll(head_b, HEAD_DIM // 2, 1)).astype(BF16)
            o_ref[:, c + LANE:c + 2 * LANE] = jnp.where(
                low, pltpu.roll(head_a, HEAD_DIM // 2, 1), head_b).astype(BF16)

    @pl.when(pl.program_id(0) >= n_pair_tiles)
    def _():
        o_ref[...] = w_ref[...].astype(BF16)


def _qkv_weights(w, *, layer, tn):
    _, d, n = w.shape
    nt = n // tn
    return pl.pallas_call(
        functools.partial(_qkv_weights_kernel, n_pair_tiles=2 * nt // 3),
        grid=(nt,),
        in_specs=[pl.BlockSpec((None, d, tn), lambda j: (layer, 0, j))],
        out_specs=pl.BlockSpec((None, d, tn), lambda j: (j, 0, 0)),
        out_shape=jax.ShapeDtypeStruct((nt, d, tn), BF16),
        compiler_params=_params("parallel"),
        name="qkv_weights",
    )(w)


def _ffn(x2d, gain, wg, wu, wd, *, layer, tm):
    t, d = x2d.shape
    _, nf, _, tf = wg.shape
    return pl.pallas_call(
        functools.partial(_ffn_kernel, d_model=d),
        grid=(t // tm, nf),
        in_specs=[pl.BlockSpec((tm, d), lambda i, j: (i, 0)), pl.BlockSpec((1, d), lambda i, j: (0, 0)),
                  pl.BlockSpec((None, None, d, tf), lambda i, j: (layer, j, 0, 0)),
                  pl.BlockSpec((None, None, d, tf), lambda i, j: (layer, j, 0, 0)),
                  pl.BlockSpec((None, tf, d), lambda i, j: (layer, j, 0))],
        out_specs=pl.BlockSpec((tm, d), lambda i, j: (i, 0)),
        out_shape=jax.ShapeDtypeStruct((t, d), F32),
        scratch_shapes=[pltpu.VMEM((tm, d), BF16)],
        compiler_params=_params("parallel", "arbitrary"),
        name="ffn",
    )(x2d, gain, wg, wu, wd)


def _dil_prep_kernel(x_ref, pos_ref, invf_ref, sgn_ref, g_ref, *refs, dilations, d_model):
    n_out = 3 * len(dilations)
    outs, (h_s, cos_s, sin_s) = refs[:n_out], refs[n_out:]
    tm = x_ref.shape[0]
    n_tiles = d_model // LANE
    x = x_ref[...]
    h = x * _rms_scale(x, d_model) * g_ref[...]
    for c in range(n_tiles):
        h_s[c] = h[:, c * LANE:(c + 1) * LANE]
    ang = pos_ref[...] * invf_ref[...]
    cos_s[...] = jnp.cos(ang)
    sin_s[...] = jnp.sin(ang) * sgn_ref[...]
    for gi, d in enumerate(dilations):
        h_o, cos_o, sin_o = outs[3 * gi:3 * gi + 3]
        if d == 1:
            h_o[...] = h.astype(BF16)
            cos_o[...] = cos_s[...]
            sin_o[...] = sin_s[...]
        else:
            for r in range(d):
                rows = pl.ds(r, tm // d, stride=d)
                for c in range(n_tiles):
                    h_o[r, :, c * LANE:(c + 1) * LANE] = h_s[c, rows, :].astype(BF16)
                cos_o[r] = cos_s[rows, :]
                sin_o[r] = sin_s[rows, :]


def _dil_prep(x2d, pos, invf, sgn, gain, *, dilations, seq, tm):
    t, d_model = x2d.shape
    b = t // seq
    nbl = seq // tm
    row = lambda i: (i, 0)
    const = lambda i: (0, 0)
    out_specs, out_shape = [], []
    for d in dilations:
        for width, dtype in ((d_model, BF16), (LANE, F32), (LANE, F32)):
            if d == 1:
                out_specs.append(pl.BlockSpec((tm, width), row))
                out_shape.append(jax.ShapeDtypeStruct((t, width), dtype))
            else:
                out_specs.append(pl.BlockSpec((None, d, tm // d, width), lambda i: (i // nbl, 0, i % nbl, 0)))
                out_shape.append(jax.ShapeDtypeStruct((b, d, seq // d, width), dtype))
    outs = pl.pallas_call(
        functools.partial(_dil_prep_kernel, dilations=dilations, d_model=d_model),
        grid=(t // tm,),
        in_specs=[pl.BlockSpec((tm, d_model), row), pl.BlockSpec((tm, 1), row), pl.BlockSpec(invf.shape, const),
                  pl.BlockSpec(sgn.shape, const), pl.BlockSpec(gain.shape, const)],
        out_specs=out_specs,
        out_shape=out_shape,
        scratch_shapes=[pltpu.VMEM((d_model // LANE, tm, LANE), F32), pltpu.VMEM((tm, LANE), F32),
                        pltpu.VMEM((tm, LANE), F32)],
        compiler_params=_params("parallel"),
        name="dil_prep",
    )(x2d, pos, invf, sgn, gain)
    outs = [o.reshape(t, o.shape[-1]) for o in outs]
    return [tuple(outs[3 * gi:3 * gi + 3]) for gi in range(len(dilations))]


def _dil_qkv_kernel(h_ref, cos_ref, sin_ref, w_ref, gains_ref, seg_ref, o_ref, *, tn, tiles_per_region,
                    group, n_groups):
    j = pl.program_id(1)

    @pl.when(j < 2 * tiles_per_region)
    def _():
        region = (j // tiles_per_region) * n_groups + group
        g_lo = gains_ref[pl.ds(2 * region, 1), :]
        g_hi = gains_ref[pl.ds(2 * region + 1, 1), :]
        cos, sin = cos_ref[...], sin_ref[...]
        a_cos, b_sin, b_cos, a_sin = g_lo * cos, g_hi * sin, g_hi * cos, g_lo * sin
        y = jnp.dot(h_ref[...], w_ref[...], preferred_element_type=F32)
        for c in range(0, tn, 2 * MXU_DIM):
            slabs = [y[:, c + k * LANE:c + (k + 1) * LANE] for k in range(4)]
            sq = jnp.concatenate([slabs[0] * slabs[0] + slabs[1] * slabs[1],
                                  slabs[2] * slabs[2] + slabs[3] * slabs[3]], axis=-1)
            ss = jnp.dot(sq.astype(BF16), seg_ref[...], preferred_element_type=F32)
            r = lax.rsqrt(ss * (1.0 / HEAD_DIM) + EPS)
            for p in range(2):
                a, b = slabs[2 * p], slabs[2 * p + 1]
                rp = r[:, p * LANE:(p + 1) * LANE]
                base = c + p * MXU_DIM
                o_ref[:, base:base + LANE] = ((a * a_cos - b * b_sin) * rp).astype(BF16)
                o_ref[:, base + LANE:base + 2 * LANE] = ((b * b_cos + a * a_sin) * rp).astype(BF16)

    @pl.when(j >= 2 * tiles_per_region)
    def _():
        o_ref[...] = jnp.dot(h_ref[...], w_ref[...], preferred_element_type=F32).astype(BF16)


def _dil_qkv(h, cos, sin, w, gains, seg_ones, *, group, n_groups, region, tm):
    t, d = h.shape
    tn = w.shape[2]
    tpr = region // tn
    wtile = lambda i, j: (((j // tpr) * n_groups + group) * tpr + j % tpr, 0, 0)
    return pl.pallas_call(
        functools.partial(_dil_qkv_kernel, tn=tn, tiles_per_region=tpr, group=group, n_groups=n_groups),
        grid=(t // tm, 3 * tpr),
        in_specs=[pl.BlockSpec((tm, d), lambda i, j: (i, 0)), pl.BlockSpec((tm, LANE), lambda i, j: (i, 0)),
                  pl.BlockSpec((tm, LANE), lambda i, j: (i, 0)), pl.BlockSpec((None, d, tn), wtile),
                  pl.BlockSpec(gains.shape, lambda i, j: (0, 0)),
                  pl.BlockSpec(seg_ones.shape, lambda i, j: (0, 0))],
        out_specs=pl.BlockSpec((tm, tn), lambda i, j: (i, j)),
        out_shape=jax.ShapeDtypeStruct((t, 3 * region), BF16),
        compiler_params=_params("parallel", "arbitrary"),
        name=f"dil_qkv_g{group}",
    )(h, cos, sin, w, gains, seg_ones)


def _dil_attn_kernel(*refs, lb, n_heads, has_prev):
    if has_prev:
        q_ref, k_ref, v_ref, kp_ref, vp_ref, o_ref, lse_ref = refs
    else:
        q_ref, k_ref, v_ref, o_ref, lse_ref = refs
    blk = ATTN_BLK
    not_first = pl.program_id(1) > 0
    ii = lax.broadcasted_iota(jnp.int32, (blk, 2 * blk), 0)
    jj = lax.broadcasted_iota(jnp.int32, (blk, 2 * blk), 1)
    band = (jj >= ii) & (jj <= ii + DIL_SPAN)
    bias_band = jnp.where(band, 0.0, NEG_INF).astype(F32)
    bias_head = jnp.where(band & (jj >= blk), 0.0, NEG_INF).astype(F32)
    bias_tri = bias_head[:, blk:]
    lane = lax.broadcasted_iota(jnp.int32, (blk, LANE), 1)
    contract_last = (((1,), (1,)), ((), ()))

    pair_lane = lax.broadcasted_iota(jnp.int32, (blk, 2 * LANE), 1) % LANE
    lo_mask = jnp.where(pair_lane < HEAD_DIM // 2, 1.0, 0.0).astype(BF16)
    pair_mask = jnp.concatenate([lo_mask, 1.0 - lo_mask], axis=0)
    stack2 = lambda a: jnp.concatenate([a, a], axis=0)
    bias_band2, bias_head2, bias_tri2 = stack2(bias_band), stack2(bias_head), stack2(bias_tri)

    for i in range(lb // blk):
        rows = slice(i * blk, (i + 1) * blk)
        krows = slice((i - 1) * blk, (i + 1) * blk)
        lse_tile = jnp.zeros((blk, LANE), F32)
        for pr in range(n_heads // 2):
            ps = slice(pr * 2 * LANE, (pr + 1) * 2 * LANE)
            q2 = stack2(q_ref[rows, ps]) * pair_mask
            if i > 0:
                k = k_ref[krows, ps]
                bias = bias_band2
            elif has_prev:
                k = jnp.concatenate([kp_ref[:, ps], k_ref[rows, ps]], axis=0)
                bias = jnp.where(not_first, bias_band2, bias_head2)
            else:
                k = k_ref[rows, ps]
                bias = bias_tri2
            s = lax.dot_general(q2, k, contract_last, preferred_element_type=F32) + bias
            m = jnp.max(s, axis=-1, keepdims=True)
            p = jnp.exp2(s - m)
            l = jnp.sum(p, axis=-1, keepdims=True)
            lse = m + jnp.log2(l)
            p = p.astype(BF16)
            for sub in range(2):
                hd = 2 * pr + sub
                cs = slice(hd * LANE, (hd + 1) * LANE)
                hr = slice(sub * blk, (sub + 1) * blk)
                if i > 0:
                    v = v_ref[krows, cs]
                elif has_prev:
                    v = jnp.concatenate([vp_ref[:, cs], v_ref[rows, cs]], axis=0)
                else:
                    v = v_ref[rows, cs]
                o = jnp.dot(p[hr], v, preferred_element_type=F32) / l[hr]
                o_ref[rows, cs] = o.astype(BF16)
                lse_tile = jnp.where(lane == hd, lse[hr], lse_tile)
        lse_ref[rows, :] = lse_tile


def _dil_attn(qkv, *, group, dilation, n_heads, seq):
    t, n = qkv.shape
    hd = n_heads * LANE
    l = seq // dilation
    lb = min(l, 2 * ATTN_BLK)
    nsub = lb // ATTN_BLK
    nl = l // lb
    has_prev = l > lb
    cur = lambda c: pl.BlockSpec((lb, hd), lambda ni, li: (ni * nl + li, c))
    prev = lambda c: pl.BlockSpec(
        (ATTN_BLK, hd), lambda ni, li: (ni * nl * nsub + jnp.maximum(li * nsub - 1, 0), c))
    in_specs = [cur(0), cur(1), cur(2)]
    args = [qkv, qkv, qkv]
    if has_prev:
        in_specs += [prev(1), prev(2)]
        args += [qkv, qkv]
    return pl.pallas_call(
        functools.partial(_dil_attn_kernel, lb=lb, n_heads=n_heads, has_prev=has_prev),
        grid=(t // l, nl),
        in_specs=in_specs,
        out_specs=[pl.BlockSpec((lb, hd), lambda ni, li: (ni * nl + li, 0)),
                   pl.BlockSpec((lb, LANE), lambda ni, li: (ni * nl + li, 0))],
        out_shape=[jax.ShapeDtypeStruct((t, hd), BF16), jax.ShapeDtypeStruct((t, LANE), F32)],
        compiler_params=_params("parallel", "arbitrary"),
        name=f"dil_attn_g{group}",
    )(*args)


def _dil_out_kernel(*refs, n_heads, dilations, n_split):
    ng = len(dilations)
    o_refs, l_refs = refs[:ng], refs[ng:2 * ng]
    x_ref, w_ref, out_ref = refs[2 * ng:2 * ng + 3]
    scratch = list(refs[2 * ng + 3:])
    oc_ref = scratch.pop()
    tm = x_ref.shape[0]
    th = tm // n_split

    for base in range(0, tm, th):
        head_slab, lses = [], []
        for gi, (o_ref, l_ref, d) in enumerate(zip(o_refs, l_refs, dilations)):
            if d == 1:
                head_slab.append(lambda hd, o_ref=o_ref, base=base:
                                 o_ref[base:base + th, hd * LANE:(hd + 1) * LANE].astype(F32))
                lses.append(l_ref[base:base + th, :])
            else:
                si = 2 * sum(1 for dd in dilations[:gi] if dd != 1)
                on_ref, ln_ref = scratch[si], scratch[si + 1]
                src = slice(base // d, (base + th) // d)
                for r in range(d):
                    rows = pl.ds(base + r, th // d, stride=d)
                    for hd in range(n_heads):
                        on_ref[hd, rows, :] = o_ref[r, src, hd * LANE:(hd + 1) * LANE].astype(F32)
                    ln_ref[rows, :] = l_ref[r, src, :]
                head_slab.append(lambda hd, on_ref=on_ref, base=base: on_ref[hd, base:base + th, :])
                lses.append(ln_ref[base:base + th, :])

        m = functools.reduce(jnp.maximum, lses)
        es = [jnp.exp2(l - m) for l in lses]
        inv = 1.0 / functools.reduce(lambda a, c: a + c, es)
        ws = [e * inv for e in es]
        for hd in range(n_heads):
            oc = None
            for w, slab in zip(ws, head_slab):
                term = w[:, hd:hd + 1] * slab(hd)
                oc = term if oc is None else oc + term
            oc_ref[base:base + th, hd * LANE:(hd + 1) * LANE] = oc.astype(BF16)
        out_ref[base:base + th, :] = x_ref[base:base + th, :] + jnp.dot(
            oc_ref[base:base + th, :], w_ref[...], preferred_element_type=F32)


def _dil_out(os, lses, x2d, w, *, n_heads, dilations, seq, tm, n_split):
    t, d_model = x2d.shape
    b = t // seq
    hd = n_heads * LANE
    nbl = seq // tm
    row = lambda i: (i, 0)

    def spec(width, d):
        if d == 1:
            return pl.BlockSpec((tm, width), row)
        return pl.BlockSpec((None, d, tm // d, width), lambda i: (i // nbl, 0, i % nbl, 0))

    view = lambda a, d: a if d == 1 else a.reshape(b, d, seq // d, a.shape[-1])
    scratch = []
    for d in dilations:
        if d != 1:
            scratch += [pltpu.VMEM((n_heads, tm, LANE), F32), pltpu.VMEM((tm, LANE), F32)]
    scratch.append(pltpu.VMEM((tm, hd), BF16))
    return pl.pallas_call(
        functools.partial(_dil_out_kernel, n_heads=n_heads, dilations=dilations, n_split=n_split),
        grid=(t // tm,),
        in_specs=[spec(hd, d) for d in dilations] + [spec(LANE, d) for d in dilations]
        + [pl.BlockSpec((tm, d_model), row),
           pl.BlockSpec(w.shape, lambda i: (0, 0), pipeline_mode=pl.Buffered(1))],
        out_specs=pl.BlockSpec((tm, d_model), row),
        out_shape=jax.ShapeDtypeStruct((t, d_model), F32),
        scratch_shapes=scratch,
        compiler_params=_params("parallel"),
        name="dil_out",
    )(*[view(o, d) for o, d in zip(os, dilations)], *[view(l, d) for l, d in zip(lses, dilations)], x2d, w)


def _rope_rows(dim, live):
    half = dim // 2
    inv_freq = jnp.power(ROPE_THETA, -2.0 * jnp.arange(half, dtype=F32) / dim)
    lane = jnp.arange(LANE)
    invf = jnp.where(lane < live, inv_freq[lane % half], 0.0).astype(F32)
    sgn = jnp.where(lane < live, jnp.where(lane % dim < half, -1.0, 1.0), 0.0).astype(F32)
    return invf[None, :], sgn[None, :]


def _rope_slab(a):
    q = MLA_ROPE // 2
    z = jnp.zeros(a.shape[:-1] + (LANE // 2 - q,), a.dtype)
    return jnp.concatenate([a[..., :q], z, a[..., q:], z], axis=-1)


def kernel(x, positions, mixer_norm, ffn_norm, mla_w_down, mla_q_norm, mla_kv_norm, mla_w_uq, mla_w_ukv,
           mla_q_gain, mla_k_gain, mla_w_o, dil_w_qkv, dil_q_gain, dil_k_gain, dil_w_o, ffn_w_gate,
           ffn_w_up, ffn_w_down):
    b, s, d = x.shape
    t = b * s
    depth = mixer_norm.shape[0]
    n_heads = d // HEAD_DIM
    x2d = x.reshape(t, d)
    pos = positions.astype(F32).reshape(t, 1)
    wg_tiles, wu_tiles, wd_bf16 = _ffn_weights(ffn_w_gate, ffn_w_up, ffn_w_down, tf=FFN_TILE)

    for i in range(depth):
        j = i // 2
        gmix = mixer_norm[i][None, :]
        if i % 2 == 0:
            q_lora = mla_q_norm.shape[1]
            kv_lora = mla_kv_norm.shape[1]
            scale = LOG2_E / math.sqrt(HEAD_DIM + MLA_ROPE)
            invf, sgn = _rope_rows(MLA_ROPE, MLA_ROPE)
            invf, sgn = _rope_slab(invf[:, :MLA_ROPE]), _rope_slab(sgn[:, :MLA_ROPE])
            wd = mla_w_down[j]
            wd = jnp.concatenate([wd[:, :q_lora + kv_lora], _rope_slab(wd[:, q_lora + kv_lora:])],
                                 axis=1).astype(BF16)
            wuq = mla_w_uq[j].reshape(q_lora, n_heads, HEAD_DIM + MLA_ROPE)
            wuq = jnp.concatenate([wuq[..., :HEAD_DIM], _rope_slab(wuq[..., HEAD_DIM:])],
                                  axis=-1).reshape(q_lora, n_heads * 2 * LANE)
            wukv = mla_w_ukv[j].reshape(kv_lora, n_heads, 2 * HEAD_DIM)
            wukv = jnp.concatenate([wukv[:, :, :HEAD_DIM].reshape(kv_lora, n_heads * LANE),
                                    wukv[:, :, HEAD_DIM:].reshape(kv_lora, n_heads * LANE)], axis=1)
            qg = mla_q_gain[j] * scale
            kg = mla_k_gain[j]
            qcat, kn, kr, v = _mla_proj(
                x2d, pos, invf, sgn, gmix, wd, mla_q_norm[j][None, :], mla_kv_norm[j][None, :],
                wuq.astype(BF16), wukv.astype(BF16), qg[None, :HEAD_DIM], _rope_slab(qg[None, HEAD_DIM:]),
                kg[None, :HEAD_DIM], _rope_slab(kg[None, HEAD_DIM:]), n_heads=n_heads, tm=256)
            o = _mla_attn(qcat.reshape(b, s, -1), kn.reshape(b, s, -1), kr.reshape(b, s, -1),
                          v.reshape(b, s, -1), n_heads=n_heads, tq=512, tk=256, heads_per_step=2)
            x2d = _proj_resid(o.reshape(t, -1), mla_w_o[j].astype(BF16), x2d, tm=512)
        else:
            n_groups = len(DIL_PAIRS)
            dilations = tuple(dl for _, dl in DIL_PAIRS)
            scale = LOG2_E / math.sqrt(HEAD_DIM)
            half = HEAD_DIM // 2
            invf, _ = _rope_rows(HEAD_DIM, HEAD_DIM)
            sgn = jnp.ones_like(invf)
            gains = jnp.concatenate([dil_q_gain[j] * scale, dil_k_gain[j]], axis=0)
            gains = jnp.stack([jnp.tile(gains[:, :half], (1, 2)), jnp.tile(gains[:, half:], (1, 2))],
                              axis=1).reshape(4 * n_groups, LANE)
            w_qkv = _qkv_weights(dil_w_qkv, layer=j, tn=min(1024, n_heads * LANE))
            seg_ones = jnp.kron(jnp.eye(2 * MXU_DIM // HEAD_DIM, dtype=F32),
                                jnp.ones((half, half), F32)).astype(BF16)
            prepped = _dil_prep(x2d, pos, invf, sgn, gmix, dilations=dilations, seq=s, tm=512)
            os, lses = [], []
            for g, (h_g, cos_g, sin_g) in enumerate(prepped):
                qkv = _dil_qkv(h_g, cos_g, sin_g, w_qkv, gains, seg_ones, group=g, n_groups=n_groups,
                               region=n_heads * LANE, tm=min(t, 1024))
                o_g, lse_g = _dil_attn(qkv, group=g, dilation=dilations[g], n_heads=n_heads, seq=s)
                os.append(o_g)
                lses.append(lse_g)
            x2d = _dil_out(os, lses, x2d, dil_w_o[j].astype(BF16), n_heads=n_heads, dilations=dilations,
                           seq=s, tm=512, n_split=2)
        x2d = _ffn(x2d, ffn_norm[i][None, :], wg_tiles, wu_tiles, wd_bf16, layer=i, tm=min(t, 1024))
    return x2d.reshape(b, s, d)
```

```python
import functools
import math

import jax
import jax.numpy as jnp
from jax import lax
from jax.experimental import pallas as pl
from jax.experimental.pallas import tpu as pltpu

F32 = jnp.float32
BF16 = jnp.bfloat16

EPS = 1e-6
ROPE_THETA = 10000.0
NEG_INF = -1e30

LANE = 128
MXU_DIM = 256
HEAD_DIM = 128
MLA_ROPE = 64
DIL_PAIRS = ((128, 1), (512, 4), (2048, 16))
DIL_SPAN = 128
ATTN_BLK = 128
FFN_TILE = 512
BF16_SUBLANES = 16
LOG2_E = math.log2(math.e)

VMEM_LIMIT_BYTES = 56 * 1024 * 1024


def _params(*sem):
    return pltpu.CompilerParams(dimension_semantics=sem, vmem_limit_bytes=VMEM_LIMIT_BYTES)


def _rms_scale(x, width):
    return lax.rsqrt(jnp.sum(x * x, axis=-1, keepdims=True) * (1.0 / width) + EPS)


def _mla_proj_kernel(x_ref, pos_ref, invf_ref, sgn_ref, gmix_ref, wd_ref, qn_ref, kvn_ref,
                     wuq_ref, wukv_ref, qgn_ref, qgr_ref, kgn_ref, kgr_ref,
                     qcat_ref, kn_ref, kr_ref, v_ref, *, n_heads, q_lora, kv_lora, d_model, n_split):
    tm = x_ref.shape[0]
    th = tm // n_split
    qgn, qgr, kgn, kgr = qgn_ref[...], qgr_ref[...], kgn_ref[...], kgr_ref[...]
    for r0 in range(0, tm, th):
        rows = slice(r0, r0 + th)
        x = x_ref[rows, :]
        h = (x * _rms_scale(x, d_model) * gmix_ref[...]).astype(BF16)
        down = jnp.dot(h, wd_ref[...], preferred_element_type=F32)
        cq = down[:, :q_lora]
        ckv = down[:, q_lora:q_lora + kv_lora]
        kr_raw = down[:, q_lora + kv_lora:]
        cq = (cq * _rms_scale(cq, q_lora) * qn_ref[...]).astype(BF16)
        ckv = (ckv * _rms_scale(ckv, kv_lora) * kvn_ref[...]).astype(BF16)
        q = jnp.dot(cq, wuq_ref[...], preferred_element_type=F32)
        kv = jnp.dot(ckv, wukv_ref[...], preferred_element_type=F32)

        ang = pos_ref[rows, :] * invf_ref[...]
        cos = jnp.cos(ang)
        sin = jnp.sin(ang) * sgn_ref[...]

        def rope(slab, cos=cos, sin=sin):
            return slab * cos + pltpu.roll(slab, LANE // 2, 1) * sin

        for hd in range(n_heads):
            base = hd * 2 * LANE
            nope = q[:, base:base + LANE]
            qcat_ref[rows, base:base + LANE] = (nope * _rms_scale(nope, HEAD_DIM) * qgn).astype(BF16)
            rp = q[:, base + LANE:base + 2 * LANE]
            qcat_ref[rows, base + LANE:base + 2 * LANE] = rope(
                rp * _rms_scale(rp, MLA_ROPE) * qgr).astype(BF16)
            kk = kv[:, hd * LANE:(hd + 1) * LANE]
            kn_ref[rows, hd * LANE:(hd + 1) * LANE] = (kk * _rms_scale(kk, HEAD_DIM) * kgn).astype(BF16)
        kr_ref[rows, :] = rope(kr_raw * _rms_scale(kr_raw, MLA_ROPE) * kgr).astype(BF16)
        v_ref[rows, :] = kv[:, n_heads * LANE:].astype(BF16)


def _mla_proj(x2d, pos, invf, sgn, gmix, wd, qn, kvn, wuq, wukv, qgn, qgr, kgn, kgr, *, n_heads, tm, n_split):
    t, d = x2d.shape
    q_lora, kv_lora = qn.shape[1], kvn.shape[1]
    hd = n_heads * LANE
    row = lambda i: (i, 0)
    const = lambda i: (0, 0)
    full = lambda a: pl.BlockSpec(a.shape, const)
    return pl.pallas_call(
        functools.partial(_mla_proj_kernel, n_heads=n_heads, q_lora=q_lora, kv_lora=kv_lora, d_model=d,
                          n_split=n_split),
        grid=(t // tm,),
        in_specs=[pl.BlockSpec((tm, d), row), pl.BlockSpec((tm, 1), row), full(invf), full(sgn), full(gmix),
                  full(wd), full(qn), full(kvn), full(wuq), full(wukv), full(qgn), full(qgr), full(kgn),
                  full(kgr)],
        out_specs=[pl.BlockSpec((tm, 2 * hd), row), pl.BlockSpec((tm, hd), row),
                   pl.BlockSpec((tm, LANE), row), pl.BlockSpec((tm, hd), row)],
        out_shape=[jax.ShapeDtypeStruct((t, 2 * hd), BF16), jax.ShapeDtypeStruct((t, hd), BF16),
                   jax.ShapeDtypeStruct((t, LANE), BF16), jax.ShapeDtypeStruct((t, hd), BF16)],
        compiler_params=_params("parallel"),
        name="mla_proj",
    )(x2d, pos, invf, sgn, gmix, wd, qn, kvn, wuq, wukv, qgn, qgr, kgn, kgr)


def _mla_attn_kernel(q_ref, kn_ref, kr_ref, v_ref, o_ref, s_ref, p_ref, *, tq, tk, heads_per_step):
    seq = q_ref.shape[0]
    contract_last = (((1,), (1,)), ((), ()))
    k_rope = kr_ref[...]
    ones_rows = jnp.ones((BF16_SUBLANES, seq), BF16)
    for hd in range(heads_per_step):
        cs = slice(hd * LANE, (hd + 1) * LANE)
        v_t = jnp.concatenate([v_ref[:, cs].astype(F32).T.astype(BF16), ones_rows], axis=0)
        for qi in range(seq // tq):
            qrows = slice(qi * tq, (qi + 1) * tq)
            kv = (qi + 1) * tq
            q = q_ref[qrows, 2 * hd * LANE:2 * (hd + 1) * LANE]
            k = jnp.concatenate([kn_ref[0:kv, cs], k_rope[0:kv]], axis=-1)
            s_ref[0:kv, :] = lax.dot_general(k, q, contract_last, preferred_element_type=F32)

            def chunk(j):
                s = s_ref[j * tk:(j + 1) * tk, :]
                if (j + 1) * tk > qi * tq:
                    key = j * tk + lax.broadcasted_iota(jnp.int32, (tk, tq), 0)
                    qry = qi * tq + lax.broadcasted_iota(jnp.int32, (tk, tq), 1)
                    s = jnp.where(key <= qry, s, NEG_INF)
                return s

            m = None
            for j in range(kv // tk):
                mj = jnp.max(chunk(j), axis=0, keepdims=True)
                m = mj if m is None else jnp.maximum(m, mj)
            for j in range(kv // tk):
                p_ref[j * tk:(j + 1) * tk, :] = jnp.exp2(chunk(j) - m).astype(BF16)
            o_t = jnp.dot(v_t[:, :kv], p_ref[0:kv, :], preferred_element_type=F32)
            o_ref[qrows, cs] = (o_t[:HEAD_DIM] / o_t[HEAD_DIM:HEAD_DIM + 1]).T.astype(BF16)


def _mla_attn(qcat, kn, kr, v, *, n_heads, tq, tk, heads_per_step):
    b, s, _ = kn.shape
    hw = heads_per_step * LANE
    head = lambda bi, hi: (bi, 0, hi)
    return pl.pallas_call(
        functools.partial(_mla_attn_kernel, tq=tq, tk=tk, heads_per_step=heads_per_step),
        grid=(b, n_heads // heads_per_step),
        in_specs=[pl.BlockSpec((None, s, 2 * hw), head), pl.BlockSpec((None, s, hw), head),
                  pl.BlockSpec((None, s, LANE), lambda bi, hi: (bi, 0, 0)),
                  pl.BlockSpec((None, s, hw), head)],
        out_specs=pl.BlockSpec((None, s, hw), head),
        out_shape=jax.ShapeDtypeStruct((b, s, n_heads * LANE), BF16),
        scratch_shapes=[pltpu.VMEM((s, tq), F32), pltpu.VMEM((s, tq), BF16)],
        compiler_params=_params("parallel", "parallel"),
        name="mla_attn",
    )(qcat, kn, kr, v)


def _proj_resid_kernel(a_ref, w_ref, r_ref, o_ref):
    o_ref[...] = r_ref[...] + jnp.dot(a_ref[...], w_ref[...], preferred_element_type=F32)


def _proj_resid(a, w, resid, *, tm):
    t, k = a.shape
    n = w.shape[1]
    return pl.pallas_call(
        _proj_resid_kernel,
        grid=(t // tm,),
        in_specs=[pl.BlockSpec((tm, k), lambda i: (i, 0)), pl.BlockSpec((k, n), lambda i: (0, 0)),
                  pl.BlockSpec((tm, n), lambda i: (i, 0))],
        out_specs=pl.BlockSpec((tm, n), lambda i: (i, 0)),
        out_shape=jax.ShapeDtypeStruct((t, n), F32),
        compiler_params=_params("parallel"),
        name="proj_resid",
    )(a, w, resid)


def _ffn_kernel(x_ref, g_ref, wg_ref, wu_ref, wd_ref, o_ref, h_ref, *, d_model):
    @pl.when(pl.program_id(1) == 0)
    def _():
        x = x_ref[...]
        h_ref[...] = (x * _rms_scale(x, d_model) * g_ref[...]).astype(BF16)
        o_ref[...] = x

    h = h_ref[...]
    gate = jnp.dot(h, wg_ref[...], preferred_element_type=F32)
    up = jnp.dot(h, wu_ref[...], preferred_element_type=F32)
    a = (gate * jax.nn.sigmoid(gate) * up).astype(BF16)
    o_ref[...] += jnp.dot(a, wd_ref[...], preferred_element_type=F32)


def _ffn_weights_kernel(wg_ref, wu_ref, wd_ref, og_ref, ou_ref, od_ref):
    og_ref[...] = wg_ref[...].astype(BF16)
    ou_ref[...] = wu_ref[...].astype(BF16)
    od_ref[...] = wd_ref[...].astype(BF16)


def _ffn_weights(w_gate, w_up, w_down, *, tf):
    n_layers, d, f = w_gate.shape
    nf = f // tf
    col_in = pl.BlockSpec((None, d, tf), lambda l, j: (l, 0, j))
    col_out = pl.BlockSpec((None, None, d, tf), lambda l, j: (l, j, 0, 0))
    row = pl.BlockSpec((None, tf, d), lambda l, j: (l, j, 0))
    return pl.pallas_call(
        _ffn_weights_kernel,
        grid=(n_layers, nf),
        in_specs=[col_in, col_in, row],
        out_specs=[col_out, col_out, row],
        out_shape=[jax.ShapeDtypeStruct((n_layers, nf, d, tf), BF16)] * 2
        + [jax.ShapeDtypeStruct((n_layers, f, d), BF16)],
        compiler_params=_params("parallel", "parallel"),
        name="ffn_weights",
    )(w_gate, w_up, w_down)


def _qkv_weights_kernel(w_ref, o_ref, *, n_pair_tiles):
    tn = w_ref.shape[1]

    @pl.when(pl.program_id(0) < n_pair_tiles)
    def _():
        lane = lax.broadcasted_iota(jnp.int32, (w_ref.shape[0], LANE), 1)
        low = lane < HEAD_DIM // 2
        for c in range(0, tn, 2 * LANE):
            head_a = w_ref[:, c:c + LANE]
            head_b = w_ref[:, c + LANE:c + 2 * LANE]
            o_ref[:, c:c + LANE] = jnp.where(low, head_a, pltpu.roll(head_b, HEAD_DIM // 2, 1)).astype(BF16)
            o_ref[:, c + LANE:c + 2 * LANE] = jnp.where(
                low, pltpu.roll(head_a, HEAD_DIM // 2, 1), head_b).astype(BF16)

    @pl.when(pl.program_id(0) >= n_pair_tiles)
    def _():
        o_ref[...] = w_ref[...].astype(BF16)


def _qkv_weights(w, *, layer, tn):
    _, d, n = w.shape
    nt = n // tn
    return pl.pallas_call(
        functools.partial(_qkv_weights_kernel, n_pair_tiles=2 * nt // 3),
        grid=(nt,),
        in_specs=[pl.BlockSpec((None, d, tn), lambda j: (layer, 0, j))],
        out_specs=pl.BlockSpec((None, d, tn), lambda j: (j, 0, 0)),
        out_shape=jax.ShapeDtypeStruct((nt, d, tn), BF16),
        compiler_params=_params("parallel"),
        name="qkv_weights",
    )(w)


def _ffn(x2d, gain, wg, wu, wd, *, layer, tm):
    t, d = x2d.shape
    _, nf, _, tf = wg.shape
    return pl.pallas_call(
        functools.partial(_ffn_kernel, d_model=d),
        grid=(t // tm, nf),
        in_specs=[pl.BlockSpec((tm, d), lambda i, j: (i, 0)), pl.BlockSpec((1, d), lambda i, j: (0, 0)),
                  pl.BlockSpec((None, None, d, tf), lambda i, j: (layer, j, 0, 0)),
                  pl.BlockSpec((None, None, d, tf), lambda i, j: (layer, j, 0, 0)),
                  pl.BlockSpec((None, tf, d), lambda i, j: (layer, j, 0))],
        out_specs=pl.BlockSpec((tm, d), lambda i, j: (i, 0)),
        out_shape=jax.ShapeDtypeStruct((t, d), F32),
        scratch_shapes=[pltpu.VMEM((tm, d), BF16)],
        compiler_params=_params("parallel", "arbitrary"),
        name="ffn",
    )(x2d, gain, wg, wu, wd)


def _dil_prep_kernel(x_ref, pos_ref, invf_ref, sgn_ref, g_ref, *refs, dilations, d_model):
    n_out = 3 * len(dilations)
    outs, (h_s, cos_s, sin_s) = refs[:n_out], refs[n_out:]
    tm = x_ref.shape[0]
    n_tiles = d_model // LANE
    x = x_ref[...]
    h = x * _rms_scale(x, d_model) * g_ref[...]
    for c in range(n_tiles):
        h_s[c] = h[:, c * LANE:(c + 1) * LANE]
    ang = pos_ref[...] * invf_ref[...]
    cos_s[...] = jnp.cos(ang)
    sin_s[...] = jnp.sin(ang) * sgn_ref[...]
    for gi, d in enumerate(dilations):
        h_o, cos_o, sin_o = outs[3 * gi:3 * gi + 3]
        if d == 1:
            h_o[...] = h.astype(BF16)
            cos_o[...] = cos_s[...]
            sin_o[...] = sin_s[...]
        else:
            for r in range(d):
                rows = pl.ds(r, tm // d, stride=d)
                for c in range(n_tiles):
                    h_o[r, :, c * LANE:(c + 1) * LANE] = h_s[c, rows, :].astype(BF16)
                cos_o[r] = cos_s[rows, :]
                sin_o[r] = sin_s[rows, :]


def _dil_prep(x2d, pos, invf, sgn, gain, *, dilations, seq, tm):
    t, d_model = x2d.shape
    b = t // seq
    nbl = seq // tm
    row = lambda i: (i, 0)
    const = lambda i: (0, 0)
    out_specs, out_shape = [], []
    for d in dilations:
        for width, dtype in ((d_model, BF16), (LANE, F32), (LANE, F32)):
            if d == 1:
                out_specs.append(pl.BlockSpec((tm, width), row))
                out_shape.append(jax.ShapeDtypeStruct((t, width), dtype))
            else:
                out_specs.append(pl.BlockSpec((None, d, tm // d, width), lambda i: (i // nbl, 0, i % nbl, 0)))
                out_shape.append(jax.ShapeDtypeStruct((b, d, seq // d, width), dtype))
    outs = pl.pallas_call(
        functools.partial(_dil_prep_kernel, dilations=dilations, d_model=d_model),
        grid=(t // tm,),
        in_specs=[pl.BlockSpec((tm, d_model), row), pl.BlockSpec((tm, 1), row), pl.BlockSpec(invf.shape, const),
                  pl.BlockSpec(sgn.shape, const), pl.BlockSpec(gain.shape, const)],
        out_specs=out_specs,
        out_shape=out_shape,
        scratch_shapes=[pltpu.VMEM((d_model // LANE, tm, LANE), F32), pltpu.VMEM((tm, LANE), F32),
                        pltpu.VMEM((tm, LANE), F32)],
        compiler_params=_params("parallel"),
        name="dil_prep",
    )(x2d, pos, invf, sgn, gain)
    outs = [o.reshape(t, o.shape[-1]) for o in outs]
    return [tuple(outs[3 * gi:3 * gi + 3]) for gi in range(len(dilations))]


def _dil_qkv_kernel(h_ref, cos_ref, sin_ref, wq_ref, wk_ref, wv_ref, gains_ref, seg_ref, o_ref, *,
                    group, n_groups):
    tpr, _, tn = wq_ref.shape
    nct = 3 * tpr
    h = h_ref[...]
    cos, sin = cos_ref[...], sin_ref[...]
    w_tile = lambda c: (wq_ref, wk_ref, wv_ref)[c // tpr][c % tpr]
    ys, scales = {}, {}

    def matmul(c):
        ys[c] = jnp.dot(h, w_tile(c), preferred_element_type=F32)

    def segment_sums(c):
        y = ys[c]
        out = []
        for c0 in range(0, tn, 2 * MXU_DIM):
            a0, b0, a1, b1 = (y[:, c0 + k * LANE:c0 + (k + 1) * LANE] for k in range(4))
            sq = jnp.concatenate([a0 * a0 + b0 * b0, a1 * a1 + b1 * b1], axis=-1)
            ss = jnp.dot(sq.astype(BF16), seg_ref[...], preferred_element_type=F32)
            out.append(lax.rsqrt(ss * (1.0 / HEAD_DIM) + EPS))
        scales[c] = out

    def finish(c):
        y = ys.pop(c)
        cols = slice(c * tn, (c + 1) * tn)
        if c >= 2 * tpr:
            o_ref[:, cols] = y.astype(BF16)
            return
        region = (c // tpr) * n_groups + group
        g_lo = gains_ref[2 * region:2 * region + 1, :]
        g_hi = gains_ref[2 * region + 1:2 * region + 2, :]
        a_cos, b_sin, b_cos, a_sin = g_lo * cos, g_hi * sin, g_hi * cos, g_lo * sin
        for pr in range(tn // MXU_DIM):
            a = y[:, pr * MXU_DIM:pr * MXU_DIM + LANE]
            b = y[:, pr * MXU_DIM + LANE:(pr + 1) * MXU_DIM]
            rp = scales[c][pr // 2][:, (pr % 2) * LANE:(pr % 2 + 1) * LANE]
            base = c * tn + pr * MXU_DIM
            o_ref[:, base:base + LANE] = ((a * a_cos - b * b_sin) * rp).astype(BF16)
            o_ref[:, base + LANE:base + 2 * LANE] = ((b * b_cos + a * a_sin) * rp).astype(BF16)

    matmul(0)
    matmul(1)
    for c in range(nct):
        if c < 2 * tpr:
            segment_sums(c)
        if c + 2 < nct:
            matmul(c + 2)
        finish(c)


def _dil_qkv(h, cos, sin, w, gains, seg_ones, *, group, n_groups, region, tm):
    t, d = h.shape
    tn = w.shape[2]
    tpr = region // tn
    row = lambda i: (i, 0)
    const = lambda i: (0, 0)
    w_part = lambda c: pl.BlockSpec((tpr, d, tn), lambda i: (c * n_groups + group, 0, 0),
                                    pipeline_mode=pl.Buffered(1))
    return pl.pallas_call(
        functools.partial(_dil_qkv_kernel, group=group, n_groups=n_groups),
        grid=(t // tm,),
        in_specs=[pl.BlockSpec((tm, d), row), pl.BlockSpec((tm, LANE), row), pl.BlockSpec((tm, LANE), row),
                  w_part(0), w_part(1), w_part(2), pl.BlockSpec(gains.shape, const),
                  pl.BlockSpec(seg_ones.shape, const)],
        out_specs=pl.BlockSpec((tm, 3 * region), row),
        out_shape=jax.ShapeDtypeStruct((t, 3 * region), BF16),
        compiler_params=_params("parallel"),
        name=f"dil_qkv_g{group}",
    )(h, cos, sin, w, w, w, gains, seg_ones)


def _dil_attn_kernel(*refs, lb, n_heads, has_prev, multi_block_seq):
    if has_prev:
        q_ref, k_ref, v_ref, kp_ref, vp_ref, o_ref, lse_ref = refs
    else:
        q_ref, k_ref, v_ref, o_ref, lse_ref = refs
    blk = ATTN_BLK
    not_first = pl.program_id(1) > 0
    ii = lax.broadcasted_iota(jnp.int32, (blk, 2 * blk), 0)
    jj = lax.broadcasted_iota(jnp.int32, (blk, 2 * blk), 1)
    band = (jj >= ii) & (jj <= ii + DIL_SPAN)
    bias_band = jnp.where(band, 0.0, NEG_INF).astype(F32)
    bias_head = jnp.where(band & (jj >= blk), 0.0, NEG_INF).astype(F32)
    bias_tri = bias_head[:, blk:]
    lane = lax.broadcasted_iota(jnp.int32, (blk, LANE), 1)
    contract_last = (((1,), (1,)), ((), ()))

    pair_lane = lax.broadcasted_iota(jnp.int32, (blk, 2 * LANE), 1) % LANE
    lo_mask = jnp.where(pair_lane < HEAD_DIM // 2, 1.0, 0.0).astype(BF16)
    pair_mask = jnp.concatenate([lo_mask, 1.0 - lo_mask], axis=0)
    stack2 = lambda a: jnp.concatenate([a, a], axis=0)
    bias_band2, bias_head2, bias_tri2 = stack2(bias_band), stack2(bias_head), stack2(bias_tri)

    for i in range(lb // blk):
        rows = slice(i * blk, (i + 1) * blk)
        krows = slice((i - 1) * blk, (i + 1) * blk)
        lse_tile = jnp.zeros((blk, LANE), F32)
        for pr in range(n_heads // 2):
            ps = slice(pr * 2 * LANE, (pr + 1) * 2 * LANE)
            q2 = stack2(q_ref[rows, ps]) * pair_mask
            if i > 0 and multi_block_seq:
                k = k_ref[krows, ps]
                bias = bias_band2
            elif has_prev:
                k = jnp.concatenate([kp_ref[:, ps], k_ref[rows, ps]], axis=0)
                bias = jnp.where(not_first, bias_band2, bias_head2)
            else:
                k = k_ref[rows, ps]
                bias = bias_tri2
            s = lax.dot_general(q2, k, contract_last, preferred_element_type=F32) + bias
            m = jnp.max(s, axis=-1, keepdims=True)
            p = jnp.exp2(s - m)
            l = jnp.sum(p, axis=-1, keepdims=True)
            lse = m + jnp.log2(l)
            p = p.astype(BF16)
            for sub in range(2):
                hd = 2 * pr + sub
                cs = slice(hd * LANE, (hd + 1) * LANE)
                hr = slice(sub * blk, (sub + 1) * blk)
                if i > 0 and multi_block_seq:
                    v = v_ref[krows, cs]
                elif has_prev:
                    v = jnp.concatenate([vp_ref[:, cs], v_ref[rows, cs]], axis=0)
                else:
                    v = v_ref[rows, cs]
                o = jnp.dot(p[hr], v, preferred_element_type=F32) / l[hr]
                o_ref[rows, cs] = o.astype(BF16)
                lse_tile = jnp.where(lane == hd, lse[hr], lse_tile)
        lse_ref[rows, :] = lse_tile


def _dil_attn(qkv, *, group, dilation, n_heads, seq):
    t, n = qkv.shape
    hd = n_heads * LANE
    l = seq // dilation
    lb = 2 * ATTN_BLK
    nsub = lb // ATTN_BLK
    nl = max(l // lb, 1)
    has_prev = l > lb
    cur = lambda c: pl.BlockSpec((lb, hd), lambda ni, li: (ni * nl + li, c))
    prev = lambda c: pl.BlockSpec(
        (ATTN_BLK, hd), lambda ni, li: (ni * nl * nsub + jnp.maximum(li * nsub - 1, 0), c))
    in_specs = [cur(0), cur(1), cur(2)]
    args = [qkv, qkv, qkv]
    if has_prev:
        in_specs += [prev(1), prev(2)]
        args += [qkv, qkv]
    return pl.pallas_call(
        functools.partial(_dil_attn_kernel, lb=lb, n_heads=n_heads, has_prev=has_prev,
                          multi_block_seq=l > ATTN_BLK),
        grid=(t // (nl * lb), nl),
        in_specs=in_specs,
        out_specs=[pl.BlockSpec((lb, hd), lambda ni, li: (ni * nl + li, 0)),
                   pl.BlockSpec((lb, LANE), lambda ni, li: (ni * nl + li, 0))],
        out_shape=[jax.ShapeDtypeStruct((t, hd), BF16), jax.ShapeDtypeStruct((t, LANE), F32)],
        compiler_params=_params("parallel", "arbitrary"),
        name=f"dil_attn_g{group}",
    )(*args)


def _dil_out_kernel(*refs, n_heads, dilations, n_split):
    ng = len(dilations)
    o_refs, l_refs = refs[:ng], refs[ng:2 * ng]
    x_ref, w_ref, out_ref = refs[2 * ng:2 * ng + 3]
    scratch = list(refs[2 * ng + 3:])
    oc_ref = scratch.pop()
    tm = x_ref.shape[0]
    th = tm // n_split

    for base in range(0, tm, th):
        head_slab, lses = [], []
        for gi, (o_ref, l_ref, d) in enumerate(zip(o_refs, l_refs, dilations)):
            if d == 1:
                head_slab.append(lambda hd, o_ref=o_ref, base=base:
                                 o_ref[base:base + th, hd * LANE:(hd + 1) * LANE].astype(F32))
                lses.append(l_ref[base:base + th, :])
            else:
                si = 2 * sum(1 for dd in dilations[:gi] if dd != 1)
                on_ref, ln_ref = scratch[si], scratch[si + 1]
                src = slice(base // d, (base + th) // d)
                for r in range(d):
                    rows = pl.ds(base + r, th // d, stride=d)
                    for hd in range(n_heads):
                        on_ref[hd, rows, :] = o_ref[r, src, hd * LANE:(hd + 1) * LANE].astype(F32)
                    ln_ref[rows, :] = l_ref[r, src, :]
                head_slab.append(lambda hd, on_ref=on_ref, base=base: on_ref[hd, base:base + th, :])
                lses.append(ln_ref[base:base + th, :])

        m = functools.reduce(jnp.maximum, lses)
        es = [jnp.exp2(l - m) for l in lses]
        inv = 1.0 / functools.reduce(lambda a, c: a + c, es)
        ws = [e * inv for e in es]
        for hd in range(n_heads):
            first = head_slab[0](hd)
            oc = first
            for w, slab in zip(ws[1:], head_slab[1:]):
                oc = oc + w[:, hd:hd + 1] * (slab(hd) - first)
            oc_ref[base:base + th, hd * LANE:(hd + 1) * LANE] = oc.astype(BF16)
        out_ref[base:base + th, :] = x_ref[base:base + th, :] + jnp.dot(
            oc_ref[base:base + th, :], w_ref[...], preferred_element_type=F32)


def _dil_out(os, lses, x2d, w, *, n_heads, dilations, seq, tm, n_split):
    t, d_model = x2d.shape
    b = t // seq
    hd = n_heads * LANE
    nbl = seq // tm
    row = lambda i: (i, 0)

    def spec(width, d):
        if d == 1:
            return pl.BlockSpec((tm, width), row)
        return pl.BlockSpec((None, d, tm // d, width), lambda i: (i // nbl, 0, i % nbl, 0))

    view = lambda a, d: a if d == 1 else a.reshape(b, d, seq // d, a.shape[-1])
    scratch = []
    for d in dilations:
        if d != 1:
            scratch += [pltpu.VMEM((n_heads, tm, LANE), F32), pltpu.VMEM((tm, LANE), F32)]
    scratch.append(pltpu.VMEM((tm, hd), BF16))
    return pl.pallas_call(
        functools.partial(_dil_out_kernel, n_heads=n_heads, dilations=dilations, n_split=n_split),
        grid=(t // tm,),
        in_specs=[spec(hd, d) for d in dilations] + [spec(LANE, d) for d in dilations]
        + [pl.BlockSpec((tm, d_model), row),
           pl.BlockSpec(w.shape, lambda i: (0, 0), pipeline_mode=pl.Buffered(1))],
        out_specs=pl.BlockSpec((tm, d_model), row),
        out_shape=jax.ShapeDtypeStruct((t, d_model), F32),
        scratch_shapes=scratch,
        compiler_params=_params("parallel"),
        name="dil_out",
    )(*[view(o, d) for o, d in zip(os, dilations)], *[view(l, d) for l, d in zip(lses, dilations)], x2d, w)


def _rope_rows(dim, live):
    half = dim // 2
    inv_freq = jnp.power(ROPE_THETA, -2.0 * jnp.arange(half, dtype=F32) / dim)
    lane = jnp.arange(LANE)
    invf = jnp.where(lane < live, inv_freq[lane % half], 0.0).astype(F32)
    sgn = jnp.where(lane < live, jnp.where(lane % dim < half, -1.0, 1.0), 0.0).astype(F32)
    return invf[None, :], sgn[None, :]


def _rope_slab(a):
    q = MLA_ROPE // 2
    z = jnp.zeros(a.shape[:-1] + (LANE // 2 - q,), a.dtype)
    return jnp.concatenate([a[..., :q], z, a[..., q:], z], axis=-1)


def kernel(x, positions, mixer_norm, ffn_norm, mla_w_down, mla_q_norm, mla_kv_norm, mla_w_uq, mla_w_ukv,
           mla_q_gain, mla_k_gain, mla_w_o, dil_w_qkv, dil_q_gain, dil_k_gain, dil_w_o, ffn_w_gate,
           ffn_w_up, ffn_w_down):
    b, s, d = x.shape
    t = b * s
    depth = mixer_norm.shape[0]
    n_heads = d // HEAD_DIM
    x2d = x.reshape(t, d)
    pos = positions.astype(F32).reshape(t, 1)
    wg_tiles, wu_tiles, wd_bf16 = _ffn_weights(ffn_w_gate, ffn_w_up, ffn_w_down, tf=FFN_TILE)

    for i in range(depth):
        j = i // 2
        gmix = mixer_norm[i][None, :]
        if i % 2 == 0:
            q_lora = mla_q_norm.shape[1]
            kv_lora = mla_kv_norm.shape[1]
            scale = LOG2_E / math.sqrt(HEAD_DIM + MLA_ROPE)
            invf, sgn = _rope_rows(MLA_ROPE, MLA_ROPE)
            invf, sgn = _rope_slab(invf[:, :MLA_ROPE]), _rope_slab(sgn[:, :MLA_ROPE])
            wd = mla_w_down[j]
            wd = jnp.concatenate([wd[:, :q_lora + kv_lora], _rope_slab(wd[:, q_lora + kv_lora:])],
                                 axis=1).astype(BF16)
            wuq = mla_w_uq[j].reshape(q_lora, n_heads, HEAD_DIM + MLA_ROPE)
            wuq = jnp.concatenate([wuq[..., :HEAD_DIM], _rope_slab(wuq[..., HEAD_DIM:])],
                                  axis=-1).reshape(q_lora, n_heads * 2 * LANE)
            wukv = mla_w_ukv[j].reshape(kv_lora, n_heads, 2 * HEAD_DIM)
            wukv = jnp.concatenate([wukv[:, :, :HEAD_DIM].reshape(kv_lora, n_heads * LANE),
                                    wukv[:, :, HEAD_DIM:].reshape(kv_lora, n_heads * LANE)], axis=1)
            qg = mla_q_gain[j] * scale
            kg = mla_k_gain[j]
            qcat, kn, kr, v = _mla_proj(
                x2d, pos, invf, sgn, gmix, wd, mla_q_norm[j][None, :], mla_kv_norm[j][None, :],
                wuq.astype(BF16), wukv.astype(BF16), qg[None, :HEAD_DIM], _rope_slab(qg[None, HEAD_DIM:]),
                kg[None, :HEAD_DIM], _rope_slab(kg[None, HEAD_DIM:]), n_heads=n_heads, tm=512, n_split=2)
            o = _mla_attn(qcat.reshape(b, s, -1), kn.reshape(b, s, -1), kr.reshape(b, s, -1),
                          v.reshape(b, s, -1), n_heads=n_heads, tq=512, tk=256, heads_per_step=2)
            x2d = _proj_resid(o.reshape(t, -1), mla_w_o[j].astype(BF16), x2d, tm=512)
        else:
            n_groups = len(DIL_PAIRS)
            dilations = tuple(dl for _, dl in DIL_PAIRS)
            scale = LOG2_E / math.sqrt(HEAD_DIM)
            half = HEAD_DIM // 2
            invf, _ = _rope_rows(HEAD_DIM, HEAD_DIM)
            sgn = jnp.ones_like(invf)
            gains = jnp.concatenate([dil_q_gain[j] * scale, dil_k_gain[j]], axis=0)
            gains = jnp.stack([jnp.tile(gains[:, :half], (1, 2)), jnp.tile(gains[:, half:], (1, 2))],
                              axis=1).reshape(4 * n_groups, LANE)
            w_qkv = _qkv_weights(dil_w_qkv, layer=j, tn=min(1024, n_heads * LANE))
            seg_ones = jnp.kron(jnp.eye(2 * MXU_DIM // HEAD_DIM, dtype=F32),
                                jnp.ones((half, half), F32)).astype(BF16)
            prepped = _dil_prep(x2d, pos, invf, sgn, gmix, dilations=dilations, seq=s, tm=512)
            os, lses = [], []
            for g, (h_g, cos_g, sin_g) in enumerate(prepped):
                qkv = _dil_qkv(h_g, cos_g, sin_g, w_qkv, gains, seg_ones, group=g, n_groups=n_groups,
                               region=n_heads * LANE, tm=512)
                o_g, lse_g = _dil_attn(qkv, group=g, dilation=dilations[g], n_heads=n_heads, seq=s)
                os.append(o_g)
                lses.append(lse_g)
            x2d = _dil_out(os, lses, x2d, dil_w_o[j].astype(BF16), n_heads=n_heads, dilations=dilations,
                           seq=s, tm=512, n_split=2)
        x2d = _ffn(x2d, ffn_norm[i][None, :], wg_tiles, wu_tiles, wd_bf16, layer=i, tm=min(t, 1024))
    return x2d.reshape(b, s, d)
```
